```python
import jax
import jax.numpy as jnp
from jax import lax
import numpy as np


D_MODEL = 1024
BATCH = 1
SEQ = 16384
DEPTH = 2
DEC_BATCH = 2
DEC_SEQ = 8192
PAST_LEN = 128

N_META = 16
N_MIXERS = 2
N_HGRN = (DEPTH + 1) // 2
N_RWKV = DEPTH // 2
HG_HEAD_DIM = 128
HG_HEADS = D_MODEL // HG_HEAD_DIM
HG_CHUNK = 64
HG_PAD = HG_CHUNK - N_META
RW_HEAD_DIM = 64
RW_HEADS = D_MODEL // RW_HEAD_DIM
RW_DECAY_LORA = 64
RW_AAA_LORA = 64
RW_GATE_LORA = 160
RW_GN_EPS = 64e-5
D_FF = 2816
NORM_EPS = 1e-6

kernel_name = 'hgrn2_rwkv7_convglu_bidir_encoder'


def rmsnorm(x, g):
    xf = x.astype(jnp.float32)
    y = xf * lax.rsqrt(jnp.mean(xf * xf, axis=-1, keepdims=True) + NORM_EPS)
    return (y * g.astype(jnp.float32)).astype(x.dtype)


def flip_seq(t):
    return jnp.flip(t, axis=1)


def hgrn2_chunk_scan(q, k, i, logf):
    B, Lp, H, DK = q.shape
    DV = i.shape[-1]
    C = HG_CHUNK
    N = Lp // C
    rs = lambda t: t.astype(jnp.float32).reshape(B, N, C, H, t.shape[-1])
    q, k, i, logf = rs(q), rs(k), rs(i), rs(logf)
    b = jnp.cumsum(logf, axis=2)
    ref = b[:, :, C // 2:C // 2 + 1]
    b_last = b[:, :, C - 1:]
    q_in = q * jnp.exp(b - ref)
    k_in = k * jnp.exp(ref - b)
    scores = jnp.einsum('bnthk,bnshk->bnhts', q_in, k_in)
    causal = jnp.tril(jnp.ones((C, C), dtype=bool))
    scores = jnp.where(causal, scores, 0.0)
    o_intra = jnp.einsum('bnhts,bnshv->bnthv', scores, i)
    q_st = q * jnp.exp(b)
    k_st = k * jnp.exp(b_last - b)
    g_tot = jnp.exp(b_last[:, :, 0])

    def step(S, xs):
        q_n, k_n, i_n, g_n = xs
        o_n = jnp.einsum('bthk,bhkv->bthv', q_n, S)
        S = g_n[..., None] * S + jnp.einsum('bshk,bshv->bhkv', k_n, i_n)
        return S, o_n

    S0 = jnp.zeros((B, H, DK, DV), jnp.float32)
    xs = (jnp.moveaxis(q_st, 1, 0), jnp.moveaxis(k_st, 1, 0), jnp.moveaxis(i, 1, 0), jnp.moveaxis(g_tot, 1, 0))
    _, o_inter = lax.scan(step, S0, xs)
    o = o_intra + jnp.moveaxis(o_inter, 0, 1)
    return o.reshape(B, Lp, H, DV)


def hgrn2_mixer(h, norm_g, w_in, w_out, onorm_g, lb):
    B, L, D = h.shape
    H, DH = HG_HEADS, HG_HEAD_DIM
    hn = rmsnorm(h, norm_g)
    q, f_fw, f_bw, i, g = jnp.split(hn @ w_in, 5, axis=-1)
    heads = lambda t: t.reshape(B, L, H, DH)
    pad = lambda t: jnp.pad(t.astype(jnp.float32), ((0, 0), (HG_PAD, 0), (0, 0), (0, 0)))
    valid = (jnp.arange(L + HG_PAD) >= HG_PAD).astype(jnp.float32)[None, :, None, None]
    lbh = lb.astype(jnp.float32).reshape(H, DH)
    q_p = pad(heads(jax.nn.silu(q)))
    i_p = pad(heads(i))

    def gates(f):
        f = pad(heads(f))
        logf = jnp.log(lbh + (1.0 - lbh) * jax.nn.sigmoid(f)) * valid
        k = (1.0 - lbh) * jax.nn.sigmoid(-f) * valid
        return k, logf

    k_f, lf_f = gates(f_fw)
    k_b, lf_b = gates(f_bw)
    o_f = hgrn2_chunk_scan(q_p, k_f, i_p, lf_f)
    o_b = flip_seq(hgrn2_chunk_scan(flip_seq(q_p), flip_seq(k_b), flip_seq(i_p), flip_seq(lf_b)))
    o = (o_f + o_b)[:, HG_PAD:]
    o = o * lax.rsqrt(jnp.mean(o * o, axis=-1, keepdims=True) + NORM_EPS) * onorm_g.astype(jnp.float32).reshape(H, DH)
    o = o.reshape(B, L, D) * jax.nn.silu(g.astype(jnp.float32))
    return h + o.astype(h.dtype) @ w_out


def rwkv7_scan(r, w, k, v, kk, a):
    B, L, H, N = r.shape

    def step(S, xs):
        r_t, w_t, k_t, v_t, kk_t, a_t = xs
        sa = jnp.einsum('bhvk,bhk->bhv', S, kk_t)
        S = S * w_t[:, :, None, :] - sa[..., None] * (kk_t * a_t)[:, :, None, :] + v_t[..., None] * k_t[:, :, None, :]
        y = jnp.einsum('bhvk,bhk->bhv', S, r_t)
        return S, y

    S0 = jnp.zeros((B, H, N, N), jnp.float32)
    xs = (jnp.moveaxis(r, 1, 0), jnp.moveaxis(w, 1, 0), jnp.moveaxis(k, 1, 0), jnp.moveaxis(v, 1, 0), jnp.moveaxis(kk, 1, 0), jnp.moveaxis(a, 1, 0))
    _, y = lax.scan(step, S0, xs)
    return jnp.moveaxis(y, 0, 1)


def rwkv7_mixer(h, norm_g, mu, w_r, w_k, w_v, w_o, w0, w1, w2, a0, a1, a2, g1, g2, k_k, k_a, r_k, gn_w, gn_b):
    B, L, D = h.shape
    H, N = RW_HEADS, RW_HEAD_DIM
    x = rmsnorm(h, norm_g)
    xp = jnp.pad(x, ((0, 0), (1, 1), (0, 0)))
    xx = 0.5 * (xp[:, :-2] + xp[:, 2:]) - x
    xr, xw, xk, xv, xa, xg = (x + xx * mu[j] for j in range(6))
    r = xr @ w_r
    k = xk @ w_k
    v = xv @ w_v
    g = jax.nn.sigmoid(xg @ g1) @ g2
    heads = lambda t: t.astype(jnp.float32).reshape(B, L, H, N)
    rh, vh = heads(r), heads(v)
    kk = heads(k * k_k)
    kk = kk * lax.rsqrt(jnp.maximum(jnp.sum(kk * kk, axis=-1, keepdims=True), 1e-24))
    r_kh = r_k.astype(jnp.float32).reshape(H, N)

    def direction_inputs(d):
        z = (w0[d] + jnp.tanh(xw @ w1[d]) @ w2[d]).astype(jnp.float32)
        decay = jnp.exp(-jnp.exp(-jax.nn.softplus(-z) - 0.5))
        a = jax.nn.sigmoid(a0[d] + (xa @ a1[d]) @ a2[d])
        kd = heads(k * (1.0 + (a - 1.0) * k_a))
        return heads(decay), kd, heads(a)

    w_f, k_fw, a_f = direction_inputs(0)
    w_b, k_bw, a_b = direction_inputs(1)
    y_f = rwkv7_scan(rh, w_f, k_fw, vh, kk, a_f)
    y_b = flip_seq(rwkv7_scan(flip_seq(rh), flip_seq(w_b), flip_seq(k_bw), flip_seq(vh), flip_seq(kk), flip_seq(a_b)))
    y = y_f + y_b
    mean = jnp.mean(y, axis=-1, keepdims=True)
    var = jnp.mean(jnp.square(y - mean), axis=-1, keepdims=True)
    yn = ((y - mean) * lax.rsqrt(var + RW_GN_EPS)).reshape(B, L, D) * gn_w + gn_b
    bonus = (jnp.sum(rh * k_fw * r_kh, axis=-1, keepdims=True) + jnp.sum(rh * k_bw * r_kh, axis=-1, keepdims=True)) * vh
    out = (yn + bonus.reshape(B, L, D)) * g
    return h + out.astype(h.dtype) @ w_o


def conv_glu_ffn(h, norm_g, w_in, conv_w, conv_b, w_out):
    x = rmsnorm(h, norm_g)
    u, v = jnp.split(x @ w_in, 2, axis=-1)
    up = jnp.pad(u, ((0, 0), (1, 1), (0, 0)))
    u = up[:, :-2] * conv_w[0] + u * conv_w[1] + up[:, 2:] * conv_w[2] + conv_b
    return h + (jax.nn.gelu(u) * v) @ w_out


def trunk(x, p):
    B = x.shape[0]
    meta = jnp.broadcast_to(p['meta_tokens'][None].astype(x.dtype), (B, N_META, D_MODEL))
    h = jnp.concatenate([meta, x], axis=1)
    lb_all = jnp.cumsum(jax.nn.softmax(p['hg_lower_bound'].astype(jnp.float32), axis=0), axis=0)
    for layer in range(DEPTH):
        j = layer // N_MIXERS
        if layer % N_MIXERS == 0:
            h = hgrn2_mixer(h, p['norm_mix'][layer], p['hg_w_in'][j], p['hg_w_out'][j], p['hg_onorm'][j], lb_all[j])
        else:
            h = rwkv7_mixer(h, p['norm_mix'][layer], p['rw_mu'][j], p['rw_w_r'][j], p['rw_w_k'][j], p['rw_w_v'][j], p['rw_w_o'][j],
                            p['rw_w0'][j], p['rw_w1'][j], p['rw_w2'][j], p['rw_a0'][j], p['rw_a1'][j], p['rw_a2'][j],
                            p['rw_g1'][j], p['rw_g2'][j], p['rw_k_k'][j], p['rw_k_a'][j], p['rw_r_k'][j], p['rw_gn_w'][j], p['rw_gn_b'][j])
        h = conv_glu_ffn(h, p['norm_ffn'][layer], p['ffn_w_in'][layer], p['ffn_conv_w'][layer], p['ffn_conv_b'][layer], p['ffn_w_out'][layer])
    return rmsnorm(h, p['norm_final'])[:, N_META:]


def setup_inputs(seed: int = 0) -> dict:
    key = jax.random.key(seed)
    ks = jax.random.split(key, 32)
    D, F = D_MODEL, D_FF

    def nrm(idx, shape, scale):
        return jax.random.normal(ks[idx], shape, jnp.float32) * scale

    return {
        'x_prompt': nrm(0, (BATCH, SEQ, D), 1.0),
        'x_sample': nrm(1, (DEC_BATCH, DEC_SEQ, D), 1.0),
        'meta_tokens': nrm(2, (N_META, D), 1.0),
        'norm_mix': 1.0 + nrm(3, (DEPTH, D), 0.05),
        'norm_ffn': 1.0 + nrm(4, (DEPTH, D), 0.05),
        'norm_final': 1.0 + nrm(5, (D,), 0.05),
        'hg_w_in': nrm(6, (N_HGRN, D, 5 * D), D ** -0.5),
        'hg_w_out': nrm(7, (N_HGRN, D, D), D ** -0.5),
        'hg_lower_bound': nrm(8, (N_HGRN + 1, D), 0.1),
        'hg_onorm': 1.0 + nrm(9, (N_HGRN, D), 0.05),
        'rw_mu': jax.random.uniform(ks[10], (N_RWKV, 6, D), jnp.float32),
        'rw_w_r': nrm(11, (N_RWKV, D, D), D ** -0.5),
        'rw_w_k': nrm(12, (N_RWKV, D, D), D ** -0.5),
        'rw_w_v': nrm(13, (N_RWKV, D, D), D ** -0.5),
        'rw_w_o': nrm(14, (N_RWKV, D, D), D ** -0.5),
        'rw_w0': jax.random.uniform(ks[15], (N_RWKV, 2, D), jnp.float32, minval=-6.0, maxval=-1.0),
        'rw_w1': nrm(16, (N_RWKV, 2, D, RW_DECAY_LORA), 0.5 * D ** -0.5),
        'rw_w2': nrm(17, (N_RWKV, 2, RW_DECAY_LORA, D), 0.1 * RW_DECAY_LORA ** -0.5),
        'rw_a0': nrm(18, (N_RWKV, 2, D), 0.1),
        'rw_a1': nrm(19, (N_RWKV, 2, D, RW_AAA_LORA), 0.5 * D ** -0.5),
        'rw_a2': nrm(20, (N_RWKV, 2, RW_AAA_LORA, D), 0.1 * RW_AAA_LORA ** -0.5),
        'rw_g1': nrm(21, (N_RWKV, D, RW_GATE_LORA), D ** -0.5),
        'rw_g2': nrm(22, (N_RWKV, RW_GATE_LORA, D), RW_GATE_LORA ** -0.5),
        'rw_k_k': 0.85 + nrm(23, (N_RWKV, D), 0.05),
        'rw_k_a': 1.0 + nrm(24, (N_RWKV, D), 0.05),
        'rw_r_k': nrm(25, (N_RWKV, D), 0.1),
        'rw_gn_w': 1.0 + nrm(26, (N_RWKV, D), 0.05),
        'rw_gn_b': nrm(27, (N_RWKV, D), 0.02),
        'ffn_w_in': nrm(28, (DEPTH, D, 2 * F), D ** -0.5),
        'ffn_conv_w': nrm(29, (DEPTH, 3, F), 3 ** -0.5),
        'ffn_conv_b': nrm(30, (DEPTH, F), 0.02),
        'ffn_w_out': nrm(31, (DEPTH, F, D), F ** -0.5),
    }


def reference(x_prompt, x_sample, meta_tokens, norm_mix, norm_ffn, norm_final, hg_w_in, hg_w_out, hg_lower_bound, hg_onorm,
              rw_mu, rw_w_r, rw_w_k, rw_w_v, rw_w_o, rw_w0, rw_w1, rw_w2, rw_a0, rw_a1, rw_a2, rw_g1, rw_g2,
              rw_k_k, rw_k_a, rw_r_k, rw_gn_w, rw_gn_b, ffn_w_in, ffn_conv_w, ffn_conv_b, ffn_w_out):
    p = {
        'meta_tokens': meta_tokens, 'norm_mix': norm_mix, 'norm_ffn': norm_ffn, 'norm_final': norm_final,
        'hg_w_in': hg_w_in, 'hg_w_out': hg_w_out, 'hg_lower_bound': hg_lower_bound, 'hg_onorm': hg_onorm,
        'rw_mu': rw_mu, 'rw_w_r': rw_w_r, 'rw_w_k': rw_w_k, 'rw_w_v': rw_w_v, 'rw_w_o': rw_w_o,
        'rw_w0': rw_w0, 'rw_w1': rw_w1, 'rw_w2': rw_w2, 'rw_a0': rw_a0, 'rw_a1': rw_a1, 'rw_a2': rw_a2,
        'rw_g1': rw_g1, 'rw_g2': rw_g2, 'rw_k_k': rw_k_k, 'rw_k_a': rw_k_a, 'rw_r_k': rw_r_k,
        'rw_gn_w': rw_gn_w, 'rw_gn_b': rw_gn_b,
        'ffn_w_in': ffn_w_in, 'ffn_conv_w': ffn_conv_w, 'ffn_conv_b': ffn_conv_b, 'ffn_w_out': ffn_w_out,
    }
    y_prompt = trunk(x_prompt, p)
    y_sample = trunk(x_sample, p)
    return (y_prompt, y_sample)
```

```python
import functools

import jax
import jax.numpy as jnp
from jax import lax
from jax.experimental import pallas as pl
from jax.experimental.pallas import tpu as pltpu

F32 = jnp.float32
BF16 = jnp.bfloat16
HIGHEST = lax.Precision.HIGHEST

D_MODEL = 1024
D_FF = 2816
N_META = 16
N_MIXERS = 2
HG_HEADS = 8
HG_HEAD_DIM = 128
RW_HEADS = 16
RW_HEAD_DIM = 64
RW_GATE_LORA_PAD = 256
NORM_EPS = 1e-6
RW_GN_EPS = 64e-5
RW_DECAY_SCALE = 0.6065306597126334

LANES = 128
HALO = 8
CHUNK = 64
ROW_ALIGN = 256
VMEM_LIMIT_BYTES = 56 * 1024 * 1024

NT_DIMS = (((1,), (1,)), ((), ()))


def _const_spec(shape):
    nd = len(shape)
    return pl.BlockSpec(shape, lambda *_: (0,) * nd, pipeline_mode=pl.Buffered(1))


def _params(*sem):
    return pltpu.CompilerParams(dimension_semantics=sem, vmem_limit_bytes=VMEM_LIMIT_BYTES)


def _valid_rows(row0, n, seqs):
    r = row0 + lax.broadcasted_iota(jnp.int32, (n, 1), 0)
    m = None
    for start, length, _ in seqs:
        t = (r >= start) & (r < start + length)
        m = t if m is None else (m | t)
    return m


def _rms(x, g):
    return x * lax.rsqrt(jnp.mean(x * x, axis=-1, keepdims=True) + NORM_EPS) * g


def _bdot(a, b):
    return jnp.dot(a.astype(BF16), b.astype(BF16), preferred_element_type=F32)


def _bdot_nt(a, b):
    return lax.dot_general(a.astype(BF16), b.astype(BF16), NT_DIMS, preferred_element_type=F32)


def _split3(x):
    x1 = x.astype(BF16)
    r1 = x - x1.astype(F32)
    x2 = r1.astype(BF16)
    x3 = (r1 - x2.astype(F32)).astype(BF16)
    return x1, x2, x3


def _dot_exact_rhs(a_bf16, x):
    x1, x2, x3 = _split3(x)
    d = lambda t: jnp.dot(a_bf16, t, preferred_element_type=F32)
    return d(x1) + d(x2) + d(x3)


def _dot_exact_lhs(x, b_bf16):
    x1, x2, x3 = _split3(x)
    d = lambda t: jnp.dot(t, b_bf16, preferred_element_type=F32)
    return d(x1) + d(x2) + d(x3)


def _head_sum(x, e_ref, et_ref):
    s = _dot_exact_lhs(x, e_ref[...])
    return _dot_exact_lhs(s, et_ref[...])


def _shift_rows(xe, tm):
    n = xe.shape[0]
    prev = pltpu.roll(xe, 1, axis=0)[HALO:HALO + tm]
    nxt = pltpu.roll(xe, n - 1, axis=0)[HALO:HALO + tm]
    return prev, xe[HALO:HALO + tm], nxt


def _with_halo(hp_ref, h_ref, hn_ref, n_tiles):
    i = pl.program_id(0)
    hp = jnp.where(i > 0, hp_ref[...], 0.0)
    hn = jnp.where(i < n_tiles - 1, hn_ref[...], 0.0)
    return jnp.concatenate([hp, h_ref[...], hn], axis=0)


def _halo_specs(tm, n_rows):
    r = tm // HALO
    last = n_rows // HALO - 1
    prev = pl.BlockSpec((HALO, D_MODEL), lambda i: (jnp.maximum(i * r - 1, 0), 0))
    cur = pl.BlockSpec((tm, D_MODEL), lambda i: (i, 0))
    nxt = pl.BlockSpec((HALO, D_MODEL), lambda i: (jnp.minimum((i + 1) * r, last), 0))
    return prev, cur, nxt


def _row_spec(tm, width=D_MODEL):
    return pl.BlockSpec((tm, width), lambda i: (i, 0))


def _chunk_masks():
    t = lax.broadcasted_iota(jnp.int32, (CHUNK, CHUNK), 0)
    s = lax.broadcasted_iota(jnp.int32, (CHUNK, CHUNK), 1)
    return t, s


def _hg_in_kernel(h_ref, g_ref, w_ref, lb_ref, q_ref, kf_ref, lff_ref, kb_ref, lfb_ref, i_ref, gs_ref,
                  *, seqs, tm):
    hn = _rms(h_ref[...], g_ref[...])
    p = _bdot(hn, w_ref[...])
    valid = _valid_rows(pl.program_id(0) * tm, tm, seqs).astype(F32)
    lb = lb_ref[...]
    q = p[:, 0:D_MODEL]
    q_ref[...] = q * jax.nn.sigmoid(q)
    for f, k_ref, lf_ref in ((p[:, D_MODEL:2 * D_MODEL], kf_ref, lff_ref),
                             (p[:, 2 * D_MODEL:3 * D_MODEL], kb_ref, lfb_ref)):
        lf_ref[...] = jnp.log(lb + (1.0 - lb) * jax.nn.sigmoid(f)) * valid
        k_ref[...] = (1.0 - lb) * jax.nn.sigmoid(-f) * valid
    i_ref[...] = p[:, 3 * D_MODEL:4 * D_MODEL]
    g = p[:, 4 * D_MODEL:5 * D_MODEL]
    gs_ref[...] = g * jax.nn.sigmoid(g)


def _hg_chunk(q, k, lf, iv, st_ref, idx, cum, mask, ref_row, last_row):
    b = _dot_exact_rhs(cum, lf)
    b_ref = b[ref_row:ref_row + 1]
    b_last = b[last_row:last_row + 1]
    q_in = q * jnp.exp(b - b_ref)
    k_in = k * jnp.exp(b_ref - b)
    scores = jnp.where(mask, _bdot_nt(q_in, k_in), 0.0)
    o = _bdot(scores, iv)
    st = st_ref[idx]
    o = o + _bdot_nt(q * jnp.exp(b), st)
    k_st = k * jnp.exp(b_last - b)
    st_ref[idx] = st * jnp.exp(b_last) + _bdot(iv.T, k_st)
    return o


def _hg_scan_kernel(qf_ref, kf_ref, lff_ref, if_ref, qb_ref, kb_ref, lfb_ref, ib_ref,
                    of_ref, ob_ref, sf_ref, sb_ref, *, fwd_starts, bwd_starts):
    c = pl.program_id(0)

    def reset_at(ref, steps):
        hit = functools.reduce(jnp.logical_or, [c == s for s in steps])

        @pl.when(hit)
        def _():
            ref[...] = jnp.zeros(ref.shape, F32)

    reset_at(sf_ref, fwd_starts)
    reset_at(sb_ref, bwd_starts)
    t, s = _chunk_masks()
    cum_f = (s <= t).astype(BF16)
    cum_b = (s >= t).astype(BF16)
    for hd in range(HG_HEADS):
        sl = slice(hd * HG_HEAD_DIM, (hd + 1) * HG_HEAD_DIM)
        of_ref[:, sl] = _hg_chunk(qf_ref[:, sl], kf_ref[:, sl], lff_ref[:, sl], if_ref[:, sl],
                                  sf_ref, hd, cum_f, s <= t, CHUNK // 2, CHUNK - 1)
        ob_ref[:, sl] = _hg_chunk(qb_ref[:, sl], kb_ref[:, sl], lfb_ref[:, sl], ib_ref[:, sl],
                                  sb_ref, hd, cum_b, s >= t, CHUNK // 2 - 1, 0)


def _hg_out_kernel(of_ref, ob_ref, gs_ref, h_ref, on_ref, w_ref, out_ref, *, seqs, tm):
    o = of_ref[...] + ob_ref[...]
    parts = []
    for hd in range(HG_HEADS):
        oh = o[:, hd * HG_HEAD_DIM:(hd + 1) * HG_HEAD_DIM]
        parts.append(oh * lax.rsqrt(jnp.mean(oh * oh, axis=-1, keepdims=True) + NORM_EPS))
    on = jnp.concatenate(parts, axis=1) * on_ref[...] * gs_ref[...]
    y = h_ref[...] + _bdot(on, w_ref[...])
    valid = _valid_rows(pl.program_id(0) * tm, tm, seqs)
    out_ref[...] = jnp.where(valid, y, 0.0)


def _hgrn2_layer(h, norm_g, w_in, w_out, onorm_g, lb, seqs, tm):
    n_rows = h.shape[0]
    n_tiles = n_rows // tm
    n_chunks = n_rows // CHUNK
    row = _row_spec(tm)
    vec = _const_spec((1, D_MODEL))
    flat = jax.ShapeDtypeStruct((n_rows, D_MODEL), F32)
    q, kf, lff, kb, lfb, iv, gs = pl.pallas_call(
        functools.partial(_hg_in_kernel, seqs=seqs, tm=tm),
        grid=(n_tiles,),
        in_specs=[row, vec, _const_spec((D_MODEL, 5 * D_MODEL)), vec],
        out_specs=[row] * 7,
        out_shape=[flat] * 7,
        compiler_params=_params("parallel"),
        name="hg_in",
    )(h, norm_g.reshape(1, -1), w_in.astype(BF16), lb.reshape(1, -1))

    fwd_starts = tuple(s // CHUNK for s, _, _ in seqs)
    bwd_starts = tuple(n_chunks - (s + p) // CHUNK for s, _, p in seqs)
    cf = pl.BlockSpec((CHUNK, D_MODEL), lambda c: (c, 0))
    cb = pl.BlockSpec((CHUNK, D_MODEL), lambda c: (n_chunks - 1 - c, 0))
    state = pltpu.VMEM((HG_HEADS, HG_HEAD_DIM, HG_HEAD_DIM), F32)
    o_f, o_b = pl.pallas_call(
        functools.partial(_hg_scan_kernel, fwd_starts=fwd_starts, bwd_starts=bwd_starts),
        grid=(n_chunks,),
        in_specs=[cf, cf, cf, cf, cb, cb, cb, cb],
        out_specs=[cf, cb],
        out_shape=[flat, flat],
        scratch_shapes=[state, state],
        compiler_params=_params("arbitrary"),
        name="hg_scan",
    )(q, kf, lff, iv, q, kb, lfb, iv)

    return pl.pallas_call(
        functools.partial(_hg_out_kernel, seqs=seqs, tm=tm),
        grid=(n_tiles,),
        in_specs=[row, row, row, row, vec, _const_spec((D_MODEL, D_MODEL))],
        out_specs=row,
        out_shape=flat,
        compiler_params=_params("parallel"),
        name="hg_out",
    )(o_f, o_b, gs, h, onorm_g.reshape(1, -1), w_out.astype(BF16))


def _rw_in_kernel(hp_ref, h_ref, hn_ref, g_ref, mu_ref, wr_ref, wk_ref, wv_ref, g1_ref, g2_ref,
                  w0_ref, w1_ref, w2_ref, a0_ref, a1_ref, a2_ref, kk_ref, ka_ref, rk_ref, e_ref, et_ref,
                  r_out, v_out, kkn_out, g_out, bonus_out,
                  lwf_out, kdf_out, abf_out, lwb_out, kdb_out, abb_out, *, seqs, tm, n_tiles):
    xe = _rms(_with_halo(hp_ref, h_ref, hn_ref, n_tiles), g_ref[...])
    xm1, x, xp1 = _shift_rows(xe, tm)
    xx = 0.5 * (xm1 + xp1) - x
    mu = mu_ref[...]
    xr, xw, xk, xv, xa, xg = (x + xx * mu[j:j + 1] for j in range(6))
    r = _bdot(xr, wr_ref[...])
    k = _bdot(xk, wk_ref[...])
    v = _bdot(xv, wv_ref[...])
    g_out[...] = _bdot(jax.nn.sigmoid(_bdot(xg, g1_ref[...])), g2_ref[...])
    z = w0_ref[...] + _bdot(jnp.tanh(_bdot(xw, w1_ref[...])), w2_ref[...])
    lw = -RW_DECAY_SCALE * jax.nn.sigmoid(z)
    a = jax.nn.sigmoid(a0_ref[...] + _bdot(_bdot(xa, a1_ref[...]), a2_ref[...]))
    kkr = k * kk_ref[...]
    ssq = _head_sum(kkr * kkr, e_ref, et_ref)
    kkn = kkr * lax.rsqrt(jnp.maximum(ssq, 1e-24))
    valid = _valid_rows(pl.program_id(0) * tm, tm, seqs).astype(F32)
    ka = ka_ref[...]
    kd_sum = None
    for d, (lw_o, kd_o, ab_o) in enumerate(((lwf_out, kdf_out, abf_out), (lwb_out, kdb_out, abb_out))):
        a_d = a[:, d * D_MODEL:(d + 1) * D_MODEL]
        kd = k * (1.0 + (a_d - 1.0) * ka)
        kd_sum = kd if kd_sum is None else kd_sum + kd
        lw_o[...] = lw[:, d * D_MODEL:(d + 1) * D_MODEL]
        kd_o[...] = kd * valid
        ab_o[...] = kkn * a_d
    r_out[...] = r
    v_out[...] = v
    kkn_out[...] = kkn
    bonus_out[...] = _head_sum(r * kd_sum * rk_ref[...], e_ref, et_ref) * v


def _rw_chunk(r, k, v, al, ab, lw, ht_ref, idx, cum, strict, incl, mid_row, last_row, lane_a, eye):
    hp = functools.partial(jnp.dot, precision=HIGHEST, preferred_element_type=F32)
    hp_nt = lambda a_, b_: lax.dot_general(a_, b_, NT_DIMS, precision=HIGHEST, preferred_element_type=F32)
    stack = lambda z_: jnp.concatenate([jnp.where(lane_a, z_, 0.0), jnp.where(lane_a, 0.0, z_)], axis=0)

    c = _dot_exact_rhs(cum, lw)
    e = c - lw
    c_mid = c[mid_row:mid_row + 1]
    c_last = c[last_row:last_row + 1]
    beta = -ab
    dec_out = jnp.exp(c_mid - c)
    lhs = jnp.concatenate([stack(al * jnp.exp(e - c_mid)), stack(r * jnp.exp(c - c_mid))], axis=0)
    rhs = jnp.concatenate([stack(beta * dec_out), stack(k * dec_out)], axis=0)
    sc = hp_nt(lhs, rhs)
    n = 2 * CHUNK
    a_ab = jnp.where(strict, sc[:n, :n], 0.0)
    a_ak = jnp.where(strict, sc[:n, n:], 0.0)
    a_rb = jnp.where(incl, sc[n:, :n], 0.0)
    a_rk = jnp.where(incl, sc[n:, n:], 0.0)
    inv = eye + a_ab
    pw = a_ab
    for _ in range(5):
        pw = hp(pw, pw)
        inv = inv + hp(inv, pw)
    vs = stack(v)
    ht = ht_ref[idx]
    u = hp(inv, hp_nt(stack(al * jnp.exp(e)), ht) + hp(a_ak, vs))
    y = hp_nt(stack(r * jnp.exp(c)), ht) + hp(a_rb, u) + hp(a_rk, vs)
    dec_end = jnp.exp(c_last - c)
    ht_ref[idx] = ht * jnp.exp(c_last) + hp(u.T, stack(beta * dec_end)) + hp(vs.T, stack(k * dec_end))
    return y[:CHUNK] + y[CHUNK:]


def _rw_scan_kernel(rf_ref, kf_ref, vf_ref, alf_ref, abf_ref, lwf_ref,
                    rb_ref, kb_ref, vb_ref, alb_ref, abb_ref, lwb_ref,
                    yf_ref, yb_ref, hf_ref, hb_ref, *, fwd_starts, bwd_starts, pairs):
    c = pl.program_id(1)

    def reset_at(ref, steps):
        hit = functools.reduce(jnp.logical_or, [c == s for s in steps])

        @pl.when(hit)
        def _():
            ref[...] = jnp.zeros(ref.shape, F32)

    reset_at(hf_ref, fwd_starts)
    reset_at(hb_ref, bwd_starts)
    t, s = _chunk_masks()
    cum_f = (s <= t).astype(BF16)
    cum_b = (s >= t).astype(BF16)
    n = 2 * CHUNK
    ti = lax.broadcasted_iota(jnp.int32, (n, n), 0)
    si = lax.broadcasted_iota(jnp.int32, (n, n), 1)
    same = (ti // CHUNK) == (si // CHUNK)
    tt, ss = ti % CHUNK, si % CHUNK
    eye = (ti == si).astype(F32)
    lane_a = lax.broadcasted_iota(jnp.int32, (CHUNK, LANES), 1) < RW_HEAD_DIM
    for p in range(pairs):
        sl = slice(p * LANES, (p + 1) * LANES)
        yf_ref[:, sl] = _rw_chunk(rf_ref[:, sl], kf_ref[:, sl], vf_ref[:, sl], alf_ref[:, sl], abf_ref[:, sl],
                                  lwf_ref[:, sl], hf_ref, p, cum_f, same & (ss < tt), same & (ss <= tt),
                                  CHUNK // 2, CHUNK - 1, lane_a, eye)
        yb_ref[:, sl] = _rw_chunk(rb_ref[:, sl], kb_ref[:, sl], vb_ref[:, sl], alb_ref[:, sl], abb_ref[:, sl],
                                  lwb_ref[:, sl], hb_ref, p, cum_b, same & (ss > tt), same & (ss >= tt),
                                  CHUNK // 2 - 1, 0, lane_a, eye)


def _rw_out_kernel(yf_ref, yb_ref, bonus_ref, g_ref, h_ref, gnw_ref, gnb_ref, e_ref, et_ref, w_ref, out_ref,
                   *, seqs, tm):
    y = yf_ref[...] + yb_ref[...]
    inv_n = 1.0 / RW_HEAD_DIM
    mean = _head_sum(y, e_ref, et_ref) * inv_n
    dlt = y - mean
    var = _head_sum(dlt * dlt, e_ref, et_ref) * inv_n
    yn = dlt * lax.rsqrt(var + RW_GN_EPS) * gnw_ref[...] + gnb_ref[...]
    out = (yn + bonus_ref[...]) * g_ref[...]
    res = h_ref[...] + _bdot(out, w_ref[...])
    valid = _valid_rows(pl.program_id(0) * tm, tm, seqs)
    out_ref[...] = jnp.where(valid, res, 0.0)


def _rwkv7_layer(h, norm_g, mu, w_r, w_k, w_v, w_o, w0, w1, w2, a0, a1, a2, g1, g2, k_k, k_a, r_k, gn_w, gn_b,
                 seqs, tm_in, tm_out, pairs):
    n_rows = h.shape[0]
    n_chunks = n_rows // CHUNK
    flat = jax.ShapeDtypeStruct((n_rows, D_MODEL), F32)
    vec = _const_spec((1, D_MODEL))
    vec2 = _const_spec((1, 2 * D_MODEL))
    row1 = lambda t: t.reshape(1, -1)

    def cat_dirs(w):
        return jnp.concatenate([w[0], w[1]], axis=1).astype(BF16)

    def blockdiag_dirs(w):
        z = jnp.zeros_like(w[0])
        return jnp.concatenate([jnp.concatenate([w[0], z], axis=1),
                                jnp.concatenate([z, w[1]], axis=1)], axis=0).astype(BF16)

    lora = g1.shape[1]
    g1p = jnp.pad(g1, ((0, 0), (0, RW_GATE_LORA_PAD - lora))).astype(BF16)
    g2p = jnp.pad(g2, ((0, RW_GATE_LORA_PAD - lora), (0, 0))).astype(BF16)
    w1c, w2c, a1c, a2c = cat_dirs(w1), blockdiag_dirs(w2), cat_dirs(a1), blockdiag_dirs(a2)
    e = (lax.broadcasted_iota(jnp.int32, (D_MODEL, LANES), 0) // RW_HEAD_DIM
         == lax.broadcasted_iota(jnp.int32, (D_MODEL, LANES), 1)).astype(BF16)
    et = e.T

    n_tiles = n_rows // tm_in
    prev, cur, nxt = _halo_specs(tm_in, n_rows)
    row = _row_spec(tm_in)
    sq = _const_spec((D_MODEL, D_MODEL))
    outs = pl.pallas_call(
        functools.partial(_rw_in_kernel, seqs=seqs, tm=tm_in, n_tiles=n_tiles),
        grid=(n_tiles,),
        in_specs=[prev, cur, nxt, vec, _const_spec((6, D_MODEL)), sq, sq, sq,
                  _const_spec(g1p.shape), _const_spec(g2p.shape),
                  vec2, _const_spec(w1c.shape), _const_spec(w2c.shape),
                  vec2, _const_spec(a1c.shape), _const_spec(a2c.shape),
                  vec, vec, vec, _const_spec(e.shape), _const_spec(et.shape)],
        out_specs=[row] * 11,
        out_shape=[flat] * 11,
        compiler_params=_params("parallel"),
        name="rw_in",
    )(h, h, h, row1(norm_g), mu, w_r.astype(BF16), w_k.astype(BF16), w_v.astype(BF16), g1p, g2p,
      row1(w0), w1c, w2c, row1(a0), a1c, a2c, row1(k_k), row1(k_a), row1(r_k), e, et)
    r, v, kkn, g, bonus, lwf, kdf, abf, lwb, kdb, abb = outs

    width = pairs * LANES
    n_groups = D_MODEL // width
    fwd_starts = tuple(s // CHUNK for s, _, _ in seqs)
    bwd_starts = tuple(n_chunks - (s + p) // CHUNK for s, _, p in seqs)
    cf = pl.BlockSpec((CHUNK, width), lambda p, c: (c, p))
    cb = pl.BlockSpec((CHUNK, width), lambda p, c: (n_chunks - 1 - c, p))
    state = pltpu.VMEM((pairs, LANES, LANES), F32)
    y_f, y_b = pl.pallas_call(
        functools.partial(_rw_scan_kernel, fwd_starts=fwd_starts, bwd_starts=bwd_starts, pairs=pairs),
        grid=(n_groups, n_chunks),
        in_specs=[cf] * 6 + [cb] * 6,
        out_specs=[cf, cb],
        out_shape=[flat, flat],
        scratch_shapes=[state, state],
        compiler_params=_params("parallel", "arbitrary"),
        name="rw_scan",
    )(r, kdf, v, kkn, abf, lwf, r, kdb, v, kkn, abb, lwb)

    row = _row_spec(tm_out)
    return pl.pallas_call(
        functools.partial(_rw_out_kernel, seqs=seqs, tm=tm_out),
        grid=(n_rows // tm_out,),
        in_specs=[row, row, row, row, row, vec, vec, _const_spec(e.shape), _const_spec(et.shape), sq],
        out_specs=row,
        out_shape=flat,
        compiler_params=_params("parallel"),
        name="rw_out",
    )(y_f, y_b, bonus, g, h, row1(gn_w), row1(gn_b), e, et, w_o.astype(BF16))


def _ffn_kernel(hp_ref, h_ref, hn_ref, g_ref, wu_ref, wv_ref, cw_ref, cb_ref, wo_ref, gf_ref, out_ref,
                *, seqs, tm, n_tiles, final):
    he = _with_halo(hp_ref, h_ref, hn_ref, n_tiles)
    xe = _rms(he, g_ref[...])
    ue = _bdot(xe, wu_ref[...])
    v = _bdot(xe[HALO:HALO + tm], wv_ref[...])
    um1, u0, up1 = _shift_rows(ue, tm)
    cw = cw_ref[...]
    u = um1 * cw[0:1] + u0 * cw[1:2] + up1 * cw[2:3] + cb_ref[...]
    y = he[HALO:HALO + tm] + _bdot(jax.nn.gelu(u) * v, wo_ref[...])
    valid = _valid_rows(pl.program_id(0) * tm, tm, seqs)
    y = jnp.where(valid, y, 0.0)
    if final:
        y = _rms(y, gf_ref[...])
    out_ref[...] = y


def _ffn_layer(h, norm_g, w_in, conv_w, conv_b, w_out, final_g, seqs, tm, final):
    n_rows = h.shape[0]
    n_tiles = n_rows // tm
    prev, cur, nxt = _halo_specs(tm, n_rows)
    vec = _const_spec((1, D_MODEL))
    vecf = _const_spec((1, D_FF))
    return pl.pallas_call(
        functools.partial(_ffn_kernel, seqs=seqs, tm=tm, n_tiles=n_tiles, final=final),
        grid=(n_tiles,),
        in_specs=[prev, cur, nxt, vec, _const_spec((D_MODEL, D_FF)), _const_spec((D_MODEL, D_FF)),
                  _const_spec((3, D_FF)), vecf, _const_spec((D_FF, D_MODEL)), vec],
        out_specs=_row_spec(tm),
        out_shape=jax.ShapeDtypeStruct((n_rows, D_MODEL), F32),
        compiler_params=_params("parallel"),
        name="ffn",
    )(h, h, h, norm_g.reshape(1, -1), w_in[:, :D_FF].astype(BF16), w_in[:, D_FF:].astype(BF16),
      conv_w, conv_b.reshape(1, -1), w_out.astype(BF16), final_g.reshape(1, -1))


def _layout(lengths):
    seqs, start = [], 0
    for n in lengths:
        padded = -(-n // ROW_ALIGN) * ROW_ALIGN
        seqs.append((start, n, padded))
        start += padded
    return tuple(seqs), start


def _trunk(xs, p, tm=256, tm_rw_in=128, rw_pairs=1):
    seqs, n_rows = _layout([N_META + x.shape[0] for x in xs])
    meta = p['meta_tokens'].astype(F32)
    pieces = []
    for x, (_, n, padded) in zip(xs, seqs):
        pieces += [meta, x, jnp.zeros((padded - n, D_MODEL), F32)]
    h = jnp.concatenate(pieces, axis=0)
    lb_all = jnp.cumsum(jax.nn.softmax(p['hg_lower_bound'].astype(F32), axis=0), axis=0)
    depth = p['norm_mix'].shape[0]
    for layer in range(depth):
        j = layer // N_MIXERS
        if layer % N_MIXERS == 0:
            h = _hgrn2_layer(h, p['norm_mix'][layer], p['hg_w_in'][j], p['hg_w_out'][j], p['hg_onorm'][j],
                             lb_all[j], seqs, tm)
        else:
            h = _rwkv7_layer(h, p['norm_mix'][layer], p['rw_mu'][j], p['rw_w_r'][j], p['rw_w_k'][j],
                             p['rw_w_v'][j], p['rw_w_o'][j], p['rw_w0'][j], p['rw_w1'][j], p['rw_w2'][j],
                             p['rw_a0'][j], p['rw_a1'][j], p['rw_a2'][j], p['rw_g1'][j], p['rw_g2'][j],
                             p['rw_k_k'][j], p['rw_k_a'][j], p['rw_r_k'][j], p['rw_gn_w'][j], p['rw_gn_b'][j],
                             seqs, tm_rw_in, tm, rw_pairs)
        h = _ffn_layer(h, p['norm_ffn'][layer], p['ffn_w_in'][layer], p['ffn_conv_w'][layer],
                       p['ffn_conv_b'][layer], p['ffn_w_out'][layer], p['norm_final'], seqs, tm,
                       final=(layer == depth - 1))
    return [h[s + N_META:s + n] for s, n, _ in seqs]


def kernel(x_prompt, x_sample, meta_tokens, norm_mix, norm_ffn, norm_final, hg_w_in, hg_w_out, hg_lower_bound,
           hg_onorm, rw_mu, rw_w_r, rw_w_k, rw_w_v, rw_w_o, rw_w0, rw_w1, rw_w2, rw_a0, rw_a1, rw_a2, rw_g1,
           rw_g2, rw_k_k, rw_k_a, rw_r_k, rw_gn_w, rw_gn_b, ffn_w_in, ffn_conv_w, ffn_conv_b, ffn_w_out):
    p = {
        'meta_tokens': meta_tokens, 'norm_mix': norm_mix, 'norm_ffn': norm_ffn, 'norm_final': norm_final,
        'hg_w_in': hg_w_in, 'hg_w_out': hg_w_out, 'hg_lower_bound': hg_lower_bound, 'hg_onorm': hg_onorm,
        'rw_mu': rw_mu, 'rw_w_r': rw_w_r, 'rw_w_k': rw_w_k, 'rw_w_v': rw_w_v, 'rw_w_o': rw_w_o,
        'rw_w0': rw_w0, 'rw_w1': rw_w1, 'rw_w2': rw_w2, 'rw_a0': rw_a0, 'rw_a1': rw_a1, 'rw_a2': rw_a2,
        'rw_g1': rw_g1, 'rw_g2': rw_g2, 'rw_k_k': rw_k_k, 'rw_k_a': rw_k_a, 'rw_r_k': rw_r_k,
        'rw_gn_w': rw_gn_w, 'rw_gn_b': rw_gn_b,
        'ffn_w_in': ffn_w_in, 'ffn_conv_w': ffn_conv_w, 'ffn_conv_b': ffn_conv_b, 'ffn_w_out': ffn_w_out,
    }
    xs = [x_prompt[b] for b in range(x_prompt.shape[0])] + [x_sample[b] for b in range(x_sample.shape[0])]
    ys = _trunk(xs, p)
    n_p = x_prompt.shape[0]
    return (jnp.stack(ys[:n_p], axis=0), jnp.stack(ys[n_p:], axis=0))
```

```python
import functools

import jax
import jax.numpy as jnp
from jax import lax
from jax.experimental import pallas as pl
from jax.experimental.pallas import tpu as pltpu

F32 = jnp.float32
BF16 = jnp.bfloat16
HIGHEST = lax.Precision.HIGHEST

D_MODEL = 1024
D_FF = 2816
N_META = 16
N_MIXERS = 2
HG_HEADS = 8
HG_HEAD_DIM = 128
RW_HEADS = 16
RW_HEAD_DIM = 64
RW_GATE_LORA_PAD = 256
NORM_EPS = 1e-6
RW_GN_EPS = 64e-5
RW_DECAY_SCALE = 0.6065306597126334

LANES = 128
HALO = 8
CHUNK = 64
ROW_ALIGN = 256
VMEM_LIMIT_BYTES = 56 * 1024 * 1024

NT_DIMS = (((1,), (1,)), ((), ()))


def _const_spec(shape):
    nd = len(shape)
    return pl.BlockSpec(shape, lambda *_: (0,) * nd, pipeline_mode=pl.Buffered(1))


def _params(*sem):
    return pltpu.CompilerParams(dimension_semantics=sem, vmem_limit_bytes=VMEM_LIMIT_BYTES)


def _valid_rows(row0, n, seqs):
    r = row0 + lax.broadcasted_iota(jnp.int32, (n, 1), 0)
    m = None
    for start, length, _ in seqs:
        t = (r >= start) & (r < start + length)
        m = t if m is None else (m | t)
    return m


def _rms(x, g):
    return x * lax.rsqrt(jnp.mean(x * x, axis=-1, keepdims=True) + NORM_EPS) * g


def _bdot(a, b):
    return jnp.dot(a.astype(BF16), b.astype(BF16), preferred_element_type=F32)


def _bdot_nt(a, b):
    return lax.dot_general(a.astype(BF16), b.astype(BF16), NT_DIMS, preferred_element_type=F32)


RW_INV_TERMS = 2


def _split(x, terms):
    hi = x.astype(BF16)
    if terms == 1:
        return (hi,)
    return hi, (x - hi.astype(F32)).astype(BF16)


def _mms(a, b, dims=None):
    if dims is None:
        d = lambda x, y: jnp.dot(x, y, preferred_element_type=F32)
    else:
        d = lambda x, y: lax.dot_general(x, y, dims, preferred_element_type=F32)
    out = d(a[0], b[0])
    if len(a) == 2 and len(b) == 2:
        out = out + (d(a[0], b[1]) + d(a[1], b[0]))
    return out


def _split3(x):
    x1 = x.astype(BF16)
    r1 = x - x1.astype(F32)
    x2 = r1.astype(BF16)
    x3 = (r1 - x2.astype(F32)).astype(BF16)
    return x1, x2, x3


def _dot_exact_rhs(a_bf16, x):
    x1, x2, x3 = _split3(x)
    d = lambda t: jnp.dot(a_bf16, t, preferred_element_type=F32)
    return d(x1) + d(x2) + d(x3)


def _dot_exact_lhs(x, b_bf16):
    x1, x2, x3 = _split3(x)
    d = lambda t: jnp.dot(t, b_bf16, preferred_element_type=F32)
    return d(x1) + d(x2) + d(x3)


def _head_sum(x, e_ref, et_ref):
    s = _dot_exact_lhs(x, e_ref[...])
    return _dot_exact_lhs(s, et_ref[...])


def _shift_rows(xe, tm):
    n = xe.shape[0]
    prev = pltpu.roll(xe, 1, axis=0)[HALO:HALO + tm]
    nxt = pltpu.roll(xe, n - 1, axis=0)[HALO:HALO + tm]
    return prev, xe[HALO:HALO + tm], nxt


def _with_halo(hp_ref, h_ref, hn_ref, n_tiles):
    i = pl.program_id(0)
    hp = jnp.where(i > 0, hp_ref[...], 0.0)
    hn = jnp.where(i < n_tiles - 1, hn_ref[...], 0.0)
    return jnp.concatenate([hp, h_ref[...], hn], axis=0)


def _halo_specs(tm, n_rows):
    r = tm // HALO
    last = n_rows // HALO - 1
    prev = pl.BlockSpec((HALO, D_MODEL), lambda i: (jnp.maximum(i * r - 1, 0), 0))
    cur = pl.BlockSpec((tm, D_MODEL), lambda i: (i, 0))
    nxt = pl.BlockSpec((HALO, D_MODEL), lambda i: (jnp.minimum((i + 1) * r, last), 0))
    return prev, cur, nxt


def _row_spec(tm, width=D_MODEL):
    return pl.BlockSpec((tm, width), lambda i: (i, 0))


def _chunk_masks():
    t = lax.broadcasted_iota(jnp.int32, (CHUNK, CHUNK), 0)
    s = lax.broadcasted_iota(jnp.int32, (CHUNK, CHUNK), 1)
    return t, s


def _hg_in_kernel(h_ref, g_ref, w_ref, lb_ref, q_ref, kf_ref, lff_ref, kb_ref, lfb_ref, i_ref, gs_ref,
                  *, seqs, tm):
    hn = _rms(h_ref[...], g_ref[...])
    p = _bdot(hn, w_ref[...])
    valid = _valid_rows(pl.program_id(0) * tm, tm, seqs).astype(F32)
    lb = lb_ref[...]
    q = p[:, 0:D_MODEL]
    q_ref[...] = q * jax.nn.sigmoid(q)
    for f, k_ref, lf_ref in ((p[:, D_MODEL:2 * D_MODEL], kf_ref, lff_ref),
                             (p[:, 2 * D_MODEL:3 * D_MODEL], kb_ref, lfb_ref)):
        lf_ref[...] = jnp.log(lb + (1.0 - lb) * jax.nn.sigmoid(f)) * valid
        k_ref[...] = (1.0 - lb) * jax.nn.sigmoid(-f) * valid
    i_ref[...] = p[:, 3 * D_MODEL:4 * D_MODEL]
    g = p[:, 4 * D_MODEL:5 * D_MODEL]
    gs_ref[...] = g * jax.nn.sigmoid(g)


def _hg_chunk(q, k, lf, iv, st_ref, idx, cum, mask, ref_row, last_row):
    b = _dot_exact_rhs(cum, lf)
    b_ref = b[ref_row:ref_row + 1]
    b_last = b[last_row:last_row + 1]
    q_in = q * jnp.exp(b - b_ref)
    k_in = k * jnp.exp(b_ref - b)
    scores = jnp.where(mask, _bdot_nt(q_in, k_in), 0.0)
    o = _bdot(scores, iv)
    st = st_ref[idx]
    o = o + _bdot_nt(q * jnp.exp(b), st)
    k_st = k * jnp.exp(b_last - b)
    st_ref[idx] = st * jnp.exp(b_last) + _bdot(iv.T, k_st)
    return o


def _hg_scan_kernel(qf_ref, kf_ref, lff_ref, if_ref, qb_ref, kb_ref, lfb_ref, ib_ref,
                    of_ref, ob_ref, sf_ref, sb_ref, *, fwd_starts, bwd_starts):
    c = pl.program_id(0)

    def reset_at(ref, steps):
        hit = functools.reduce(jnp.logical_or, [c == s for s in steps])

        @pl.when(hit)
        def _():
            ref[...] = jnp.zeros(ref.shape, F32)

    reset_at(sf_ref, fwd_starts)
    reset_at(sb_ref, bwd_starts)
    t, s = _chunk_masks()
    cum_f = (s <= t).astype(BF16)
    cum_b = (s >= t).astype(BF16)
    for hd in range(HG_HEADS):
        sl = slice(hd * HG_HEAD_DIM, (hd + 1) * HG_HEAD_DIM)
        of_ref[:, sl] = _hg_chunk(qf_ref[:, sl], kf_ref[:, sl], lff_ref[:, sl], if_ref[:, sl],
                                  sf_ref, hd, cum_f, s <= t, CHUNK // 2, CHUNK - 1)
        ob_ref[:, sl] = _hg_chunk(qb_ref[:, sl], kb_ref[:, sl], lfb_ref[:, sl], ib_ref[:, sl],
                                  sb_ref, hd, cum_b, s >= t, CHUNK // 2 - 1, 0)


def _hg_out_kernel(of_ref, ob_ref, gs_ref, h_ref, on_ref, w_ref, out_ref, *, seqs, tm):
    o = of_ref[...] + ob_ref[...]
    parts = []
    for hd in range(HG_HEADS):
        oh = o[:, hd * HG_HEAD_DIM:(hd + 1) * HG_HEAD_DIM]
        parts.append(oh * lax.rsqrt(jnp.mean(oh * oh, axis=-1, keepdims=True) + NORM_EPS))
    on = jnp.concatenate(parts, axis=1) * on_ref[...] * gs_ref[...]
    y = h_ref[...] + _bdot(on, w_ref[...])
    valid = _valid_rows(pl.program_id(0) * tm, tm, seqs)
    out_ref[...] = jnp.where(valid, y, 0.0)


def _hgrn2_layer(h, norm_g, w_in, w_out, onorm_g, lb, seqs, tm):
    n_rows = h.shape[0]
    n_tiles = n_rows // tm
    n_chunks = n_rows // CHUNK
    row = _row_spec(tm)
    vec = _const_spec((1, D_MODEL))
    flat = jax.ShapeDtypeStruct((n_rows, D_MODEL), F32)
    q, kf, lff, kb, lfb, iv, gs = pl.pallas_call(
        functools.partial(_hg_in_kernel, seqs=seqs, tm=tm),
        grid=(n_tiles,),
        in_specs=[row, vec, _const_spec((D_MODEL, 5 * D_MODEL)), vec],
        out_specs=[row] * 7,
        out_shape=[flat] * 7,
        compiler_params=_params("parallel"),
        name="hg_in",
    )(h, norm_g.reshape(1, -1), w_in.astype(BF16), lb.reshape(1, -1))

    fwd_starts = tuple(s // CHUNK for s, _, _ in seqs)
    bwd_starts = tuple(n_chunks - (s + p) // CHUNK for s, _, p in seqs)
    cf = pl.BlockSpec((CHUNK, D_MODEL), lambda c: (c, 0))
    cb = pl.BlockSpec((CHUNK, D_MODEL), lambda c: (n_chunks - 1 - c, 0))
    state = pltpu.VMEM((HG_HEADS, HG_HEAD_DIM, HG_HEAD_DIM), F32)
    o_f, o_b = pl.pallas_call(
        functools.partial(_hg_scan_kernel, fwd_starts=fwd_starts, bwd_starts=bwd_starts),
        grid=(n_chunks,),
        in_specs=[cf, cf, cf, cf, cb, cb, cb, cb],
        out_specs=[cf, cb],
        out_shape=[flat, flat],
        scratch_shapes=[state, state],
        compiler_params=_params("arbitrary"),
        name="hg_scan",
    )(q, kf, lff, iv, q, kb, lfb, iv)

    return pl.pallas_call(
        functools.partial(_hg_out_kernel, seqs=seqs, tm=tm),
        grid=(n_tiles,),
        in_specs=[row, row, row, row, vec, _const_spec((D_MODEL, D_MODEL))],
        out_specs=row,
        out_shape=flat,
        compiler_params=_params("parallel"),
        name="hg_out",
    )(o_f, o_b, gs, h, onorm_g.reshape(1, -1), w_out.astype(BF16))


def _rw_in_kernel(hp_ref, h_ref, hn_ref, g_ref, mu_ref, wr_ref, wk_ref, wv_ref, g1_ref, g2_ref,
                  w0_ref, w1_ref, w2_ref, a0_ref, a1_ref, a2_ref, kk_ref, ka_ref, rk_ref, e_ref, et_ref,
                  r_out, v_out, kkn_out, g_out, bonus_out,
                  lwf_out, kdf_out, abf_out, lwb_out, kdb_out, abb_out, *, seqs, tm, n_tiles):
    xe = _rms(_with_halo(hp_ref, h_ref, hn_ref, n_tiles), g_ref[...])
    xm1, x, xp1 = _shift_rows(xe, tm)
    xx = 0.5 * (xm1 + xp1) - x
    mu = mu_ref[...]
    xr, xw, xk, xv, xa, xg = (x + xx * mu[j:j + 1] for j in range(6))
    r = _bdot(xr, wr_ref[...])
    k = _bdot(xk, wk_ref[...])
    v = _bdot(xv, wv_ref[...])
    g_out[...] = _bdot(jax.nn.sigmoid(_bdot(xg, g1_ref[...])), g2_ref[...])
    z = w0_ref[...] + _bdot(jnp.tanh(_bdot(xw, w1_ref[...])), w2_ref[...])
    lw = -RW_DECAY_SCALE * jax.nn.sigmoid(z)
    a = jax.nn.sigmoid(a0_ref[...] + _bdot(_bdot(xa, a1_ref[...]), a2_ref[...]))
    kkr = k * kk_ref[...]
    ssq = _head_sum(kkr * kkr, e_ref, et_ref)
    kkn = kkr * lax.rsqrt(jnp.maximum(ssq, 1e-24))
    valid = _valid_rows(pl.program_id(0) * tm, tm, seqs).astype(F32)
    ka = ka_ref[...]
    kd_sum = None
    for d, (lw_o, kd_o, ab_o) in enumerate(((lwf_out, kdf_out, abf_out), (lwb_out, kdb_out, abb_out))):
        a_d = a[:, d * D_MODEL:(d + 1) * D_MODEL]
        kd = k * (1.0 + (a_d - 1.0) * ka)
        kd_sum = kd if kd_sum is None else kd_sum + kd
        lw_o[...] = lw[:, d * D_MODEL:(d + 1) * D_MODEL]
        kd_o[...] = kd * valid
        ab_o[...] = kkn * a_d
    r_out[...] = r
    v_out[...] = v
    kkn_out[...] = kkn
    bonus_out[...] = _head_sum(r * kd_sum * rk_ref[...], e_ref, et_ref) * v


def _rw_chunks(bodies, masks, eye, lane_a):
    stack = lambda z_: jnp.concatenate([jnp.where(lane_a, z_, 0.0), jnp.where(lane_a, 0.0, z_)], axis=0)
    one = lambda z_: _split(z_, 1)
    inv_terms = RW_INV_TERMS
    n = 2 * CHUNK
    for b in bodies:
        m = masks[b['dir']]
        lw = b['lw']
        c = _dot_exact_rhs(m['cum'], lw)
        e = c - lw
        c_mid = c[m['mid']:m['mid'] + 1]
        c_last = c[m['last']:m['last'] + 1]
        beta = -b['ab']
        dec_out = jnp.exp(c_mid - c)
        rhs = jnp.concatenate([stack(beta * dec_out), stack(b['k'] * dec_out)], axis=0)
        sc_a = _mms(_split(stack(b['al'] * jnp.exp(e - c_mid)), inv_terms), _split(rhs, inv_terms), NT_DIMS)
        sc_r = _mms(one(stack(b['r'] * jnp.exp(c - c_mid))), one(rhs), NT_DIMS)
        dec_end = jnp.exp(c_last - c)
        b.update(
            a_d=jnp.where(m['d16'], sc_a[:, :n], 0.0), a_o32=jnp.where(m['o32'], sc_a[:, :n], 0.0),
            a_o64=jnp.where(m['o64'], sc_a[:, :n], 0.0), a_ak=jnp.where(m['strict'], sc_a[:, n:], 0.0),
            a_rb=jnp.where(m['incl'], sc_r[:, :n], 0.0), a_rk=jnp.where(m['incl'], sc_r[:, n:], 0.0),
            vs=stack(b['v']), al_h=stack(b['al'] * jnp.exp(e)), r_h=stack(b['r'] * jnp.exp(c)),
            b_h=stack(beta * dec_end), k_h=stack(b['k'] * dec_end), g=jnp.exp(c_last))
    for b in bodies:
        b['ht'] = b['ht_ref'][b['idx']]
        b['x'] = _mms(one(b['al_h']), one(b['ht']), NT_DIMS) + _mms(one(b['a_ak']), one(b['vs']))
    for b in bodies:
        b['pw'] = _split(b['a_d'], inv_terms)
        b['inv'] = eye + b['a_d']
    for _ in range(3):
        for b in bodies:
            b['pw'] = _split(_mms(b['pw'], b['pw']), inv_terms)
        for b in bodies:
            b['inv'] = b['inv'] + _mms(_split(b['inv'], inv_terms), b['pw'])
    for key in ('a_o32', 'a_o64'):
        for b in bodies:
            b['inv_s'] = _split(b['inv'], inv_terms)
            b['t'] = _mms(_split(b[key], inv_terms), b['inv_s'])
        for b in bodies:
            b['inv'] = b['inv'] + _mms(b['inv_s'], _split(b['t'], inv_terms))
    for b in bodies:
        b['u'] = _mms(one(b['inv']), one(b['x']))
    for b in bodies:
        y = (_mms(one(b['r_h']), one(b['ht']), NT_DIMS) + _mms(one(b['a_rb']), one(b['u']))
             + _mms(one(b['a_rk']), one(b['vs'])))
        b['y_ref'][:, b['sl']] = y[:CHUNK] + y[CHUNK:]
    for b in bodies:
        b['ht_ref'][b['idx']] = (b['ht'] * b['g'] + _mms(one(b['u'].T), one(b['b_h']))
                                 + _mms(one(b['vs'].T), one(b['k_h'])))


def _rw_scan_kernel(rf_ref, kf_ref, vf_ref, alf_ref, abf_ref, lwf_ref,
                    rb_ref, kb_ref, vb_ref, alb_ref, abb_ref, lwb_ref,
                    yf_ref, yb_ref, hf_ref, hb_ref, *, fwd_starts, bwd_starts, pairs):
    c = pl.program_id(1)

    def reset_at(ref, steps):
        hit = functools.reduce(jnp.logical_or, [c == s for s in steps])

        @pl.when(hit)
        def _():
            ref[...] = jnp.zeros(ref.shape, F32)

    reset_at(hf_ref, fwd_starts)
    reset_at(hb_ref, bwd_starts)
    t, s = _chunk_masks()
    n = 2 * CHUNK
    ti = lax.broadcasted_iota(jnp.int32, (n, n), 0)
    si = lax.broadcasted_iota(jnp.int32, (n, n), 1)
    same = lambda width: (ti // width) == (si // width)
    eye = (ti == si).astype(F32)
    lane_a = lax.broadcasted_iota(jnp.int32, (CHUNK, LANES), 1) < RW_HEAD_DIM

    def dir_masks(before, before_incl, cum, mid, last):
        return dict(cum=cum.astype(BF16), mid=mid, last=last,
                    strict=same(CHUNK) & before, incl=same(CHUNK) & before_incl,
                    d16=same(16) & before, o32=same(32) & ~same(16) & before,
                    o64=same(CHUNK) & ~same(32) & before)

    masks = {'f': dir_masks(si < ti, si <= ti, s <= t, CHUNK // 2, CHUNK - 1),
             'b': dir_masks(si > ti, si >= ti, s >= t, CHUNK // 2 - 1, 0)}
    bodies = []
    for p in range(pairs):
        sl = slice(p * LANES, (p + 1) * LANES)
        bodies.append(dict(dir='f', r=rf_ref[:, sl], k=kf_ref[:, sl], v=vf_ref[:, sl], al=alf_ref[:, sl],
                           ab=abf_ref[:, sl], lw=lwf_ref[:, sl], ht_ref=hf_ref, idx=p, y_ref=yf_ref, sl=sl))
        bodies.append(dict(dir='b', r=rb_ref[:, sl], k=kb_ref[:, sl], v=vb_ref[:, sl], al=alb_ref[:, sl],
                           ab=abb_ref[:, sl], lw=lwb_ref[:, sl], ht_ref=hb_ref, idx=p, y_ref=yb_ref, sl=sl))
    _rw_chunks(bodies, masks, eye, lane_a)


def _rw_out_kernel(yf_ref, yb_ref, bonus_ref, g_ref, h_ref, gnw_ref, gnb_ref, e_ref, et_ref, w_ref, out_ref,
                   *, seqs, tm):
    y = yf_ref[...] + yb_ref[...]
    inv_n = 1.0 / RW_HEAD_DIM
    mean = _head_sum(y, e_ref, et_ref) * inv_n
    dlt = y - mean
    var = _head_sum(dlt * dlt, e_ref, et_ref) * inv_n
    yn = dlt * lax.rsqrt(var + RW_GN_EPS) * gnw_ref[...] + gnb_ref[...]
    out = (yn + bonus_ref[...]) * g_ref[...]
    res = h_ref[...] + _bdot(out, w_ref[...])
    valid = _valid_rows(pl.program_id(0) * tm, tm, seqs)
    out_ref[...] = jnp.where(valid, res, 0.0)


def _rwkv7_layer(h, norm_g, mu, w_r, w_k, w_v, w_o, w0, w1, w2, a0, a1, a2, g1, g2, k_k, k_a, r_k, gn_w, gn_b,
                 seqs, tm_in, tm_out, pairs):
    n_rows = h.shape[0]
    n_chunks = n_rows // CHUNK
    flat = jax.ShapeDtypeStruct((n_rows, D_MODEL), F32)
    vec = _const_spec((1, D_MODEL))
    vec2 = _const_spec((1, 2 * D_MODEL))
    row1 = lambda t: t.reshape(1, -1)

    def cat_dirs(w):
        return jnp.concatenate([w[0], w[1]], axis=1).astype(BF16)

    def blockdiag_dirs(w):
        z = jnp.zeros_like(w[0])
        return jnp.concatenate([jnp.concatenate([w[0], z], axis=1),
                                jnp.concatenate([z, w[1]], axis=1)], axis=0).astype(BF16)

    lora = g1.shape[1]
    g1p = jnp.pad(g1, ((0, 0), (0, RW_GATE_LORA_PAD - lora))).astype(BF16)
    g2p = jnp.pad(g2, ((0, RW_GATE_LORA_PAD - lora), (0, 0))).astype(BF16)
    w1c, w2c, a1c, a2c = cat_dirs(w1), blockdiag_dirs(w2), cat_dirs(a1), blockdiag_dirs(a2)
    e = (lax.broadcasted_iota(jnp.int32, (D_MODEL, LANES), 0) // RW_HEAD_DIM
         == lax.broadcasted_iota(jnp.int32, (D_MODEL, LANES), 1)).astype(BF16)
    et = e.T

    n_tiles = n_rows // tm_in
    prev, cur, nxt = _halo_specs(tm_in, n_rows)
    row = _row_spec(tm_in)
    sq = _const_spec((D_MODEL, D_MODEL))
    outs = pl.pallas_call(
        functools.partial(_rw_in_kernel, seqs=seqs, tm=tm_in, n_tiles=n_tiles),
        grid=(n_tiles,),
        in_specs=[prev, cur, nxt, vec, _const_spec((6, D_MODEL)), sq, sq, sq,
                  _const_spec(g1p.shape), _const_spec(g2p.shape),
                  vec2, _const_spec(w1c.shape), _const_spec(w2c.shape),
                  vec2, _const_spec(a1c.shape), _const_spec(a2c.shape),
                  vec, vec, vec, _const_spec(e.shape), _const_spec(et.shape)],
        out_specs=[row] * 11,
        out_shape=[flat] * 11,
        compiler_params=_params("parallel"),
        name="rw_in",
    )(h, h, h, row1(norm_g), mu, w_r.astype(BF16), w_k.astype(BF16), w_v.astype(BF16), g1p, g2p,
      row1(w0), w1c, w2c, row1(a0), a1c, a2c, row1(k_k), row1(k_a), row1(r_k), e, et)
    r, v, kkn, g, bonus, lwf, kdf, abf, lwb, kdb, abb = outs

    width = pairs * LANES
    n_groups = D_MODEL // width
    fwd_starts = tuple(s // CHUNK for s, _, _ in seqs)
    bwd_starts = tuple(n_chunks - (s + p) // CHUNK for s, _, p in seqs)
    cf = pl.BlockSpec((CHUNK, width), lambda p, c: (c, p))
    cb = pl.BlockSpec((CHUNK, width), lambda p, c: (n_chunks - 1 - c, p))
    state = pltpu.VMEM((pairs, LANES, LANES), F32)
    y_f, y_b = pl.pallas_call(
        functools.partial(_rw_scan_kernel, fwd_starts=fwd_starts, bwd_starts=bwd_starts, pairs=pairs),
        grid=(n_groups, n_chunks),
        in_specs=[cf] * 6 + [cb] * 6,
        out_specs=[cf, cb],
        out_shape=[flat, flat],
        scratch_shapes=[state, state],
        compiler_params=_params("parallel", "arbitrary"),
        name="rw_scan",
    )(r, kdf, v, kkn, abf, lwf, r, kdb, v, kkn, abb, lwb)

    row = _row_spec(tm_out)
    return pl.pallas_call(
        functools.partial(_rw_out_kernel, seqs=seqs, tm=tm_out),
        grid=(n_rows // tm_out,),
        in_specs=[row, row, row, row, row, vec, vec, _const_spec(e.shape), _const_spec(et.shape), sq],
        out_specs=row,
        out_shape=flat,
        compiler_params=_params("parallel"),
        name="rw_out",
    )(y_f, y_b, bonus, g, h, row1(gn_w), row1(gn_b), e, et, w_o.astype(BF16))


def _ffn_kernel(hp_ref, h_ref, hn_ref, g_ref, wu_ref, wv_ref, cw_ref, cb_ref, wo_ref, gf_ref, out_ref,
                *, seqs, tm, n_tiles, final):
    he = _with_halo(hp_ref, h_ref, hn_ref, n_tiles)
    xe = _rms(he, g_ref[...])
    ue = _bdot(xe, wu_ref[...])
    v = _bdot(xe[HALO:HALO + tm], wv_ref[...])
    um1, u0, up1 = _shift_rows(ue, tm)
    cw = cw_ref[...]
    u = um1 * cw[0:1] + u0 * cw[1:2] + up1 * cw[2:3] + cb_ref[...]
    y = he[HALO:HALO + tm] + _bdot(jax.nn.gelu(u) * v, wo_ref[...])
    valid = _valid_rows(pl.program_id(0) * tm, tm, seqs)
    y = jnp.where(valid, y, 0.0)
    if final:
        y = _rms(y, gf_ref[...])
    out_ref[...] = y


def _ffn_layer(h, norm_g, w_in, conv_w, conv_b, w_out, final_g, seqs, tm, final):
    n_rows = h.shape[0]
    n_tiles = n_rows // tm
    prev, cur, nxt = _halo_specs(tm, n_rows)
    vec = _const_spec((1, D_MODEL))
    vecf = _const_spec((1, D_FF))
    return pl.pallas_call(
        functools.partial(_ffn_kernel, seqs=seqs, tm=tm, n_tiles=n_tiles, final=final),
        grid=(n_tiles,),
        in_specs=[prev, cur, nxt, vec, _const_spec((D_MODEL, D_FF)), _const_spec((D_MODEL, D_FF)),
                  _const_spec((3, D_FF)), vecf, _const_spec((D_FF, D_MODEL)), vec],
        out_specs=_row_spec(tm),
        out_shape=jax.ShapeDtypeStruct((n_rows, D_MODEL), F32),
        compiler_params=_params("parallel"),
        name="ffn",
    )(h, h, h, norm_g.reshape(1, -1), w_in[:, :D_FF].astype(BF16), w_in[:, D_FF:].astype(BF16),
      conv_w, conv_b.reshape(1, -1), w_out.astype(BF16), final_g.reshape(1, -1))


def _layout(lengths):
    seqs, start = [], 0
    for n in lengths:
        padded = -(-n // ROW_ALIGN) * ROW_ALIGN
        seqs.append((start, n, padded))
        start += padded
    return tuple(seqs), start


def _trunk(xs, p, tm=256, tm_rw_in=128, rw_pairs=4):
    seqs, n_rows = _layout([N_META + x.shape[0] for x in xs])
    meta = p['meta_tokens'].astype(F32)
    pieces = []
    for x, (_, n, padded) in zip(xs, seqs):
        pieces += [meta, x, jnp.zeros((padded - n, D_MODEL), F32)]
    h = jnp.concatenate(pieces, axis=0)
    lb_all = jnp.cumsum(jax.nn.softmax(p['hg_lower_bound'].astype(F32), axis=0), axis=0)
    depth = p['norm_mix'].shape[0]
    for layer in range(depth):
        j = layer // N_MIXERS
        if layer % N_MIXERS == 0:
            h = _hgrn2_layer(h, p['norm_mix'][layer], p['hg_w_in'][j], p['hg_w_out'][j], p['hg_onorm'][j],
                             lb_all[j], seqs, tm)
        else:
            h = _rwkv7_layer(h, p['norm_mix'][layer], p['rw_mu'][j], p['rw_w_r'][j], p['rw_w_k'][j],
                             p['rw_w_v'][j], p['rw_w_o'][j], p['rw_w0'][j], p['rw_w1'][j], p['rw_w2'][j],
                             p['rw_a0'][j], p['rw_a1'][j], p['rw_a2'][j], p['rw_g1'][j], p['rw_g2'][j],
                             p['rw_k_k'][j], p['rw_k_a'][j], p['rw_r_k'][j], p['rw_gn_w'][j], p['rw_gn_b'][j],
                             seqs, tm_rw_in, tm, rw_pairs)
        h = _ffn_layer(h, p['norm_ffn'][layer], p['ffn_w_in'][layer], p['ffn_conv_w'][layer],
                       p['ffn_conv_b'][layer], p['ffn_w_out'][layer], p['norm_final'], seqs, tm,
                       final=(layer == depth - 1))
    return [h[s + N_META:s + n] for s, n, _ in seqs]


def kernel(x_prompt, x_sample, meta_tokens, norm_mix, norm_ffn, norm_final, hg_w_in, hg_w_out, hg_lower_bound,
           hg_onorm, rw_mu, rw_w_r, rw_w_k, rw_w_v, rw_w_o, rw_w0, rw_w1, rw_w2, rw_a0, rw_a1, rw_a2, rw_g1,
           rw_g2, rw_k_k, rw_k_a, rw_r_k, rw_gn_w, rw_gn_b, ffn_w_in, ffn_conv_w, ffn_conv_b, ffn_w_out):
    p = {
        'meta_tokens': meta_tokens, 'norm_mix': norm_mix, 'norm_ffn': norm_ffn, 'norm_final': norm_final,
        'hg_w_in': hg_w_in, 'hg_w_out': hg_w_out, 'hg_lower_bound': hg_lower_bound, 'hg_onorm': hg_onorm,
        'rw_mu': rw_mu, 'rw_w_r': rw_w_r, 'rw_w_k': rw_w_k, 'rw_w_v': rw_w_v, 'rw_w_o': rw_w_o,
        'rw_w0': rw_w0, 'rw_w1': rw_w1, 'rw_w2': rw_w2, 'rw_a0': rw_a0, 'rw_a1': rw_a1, 'rw_a2': rw_a2,
        'rw_g1': rw_g1, 'rw_g2': rw_g2, 'rw_k_k': rw_k_k, 'rw_k_a': rw_k_a, 'rw_r_k': rw_r_k,
        'rw_gn_w': rw_gn_w, 'rw_gn_b': rw_gn_b,
        'ffn_w_in': ffn_w_in, 'ffn_conv_w': ffn_conv_w, 'ffn_conv_b': ffn_conv_b, 'ffn_w_out': ffn_w_out,
    }
    xs = [x_prompt[b] for b in range(x_prompt.shape[0])] + [x_sample[b] for b in range(x_sample.shape[0])]
    ys = _trunk(xs, p)
    n_p = x_prompt.shape[0]
    return (jnp.stack(ys[:n_p], axis=0), jnp.stack(ys[n_p:], axis=0))
```

```python
import functools

import jax
import jax.numpy as jnp
from jax import lax
from jax.experimental import pallas as pl
from jax.experimental.pallas import tpu as pltpu

F32 = jnp.float32
BF16 = jnp.bfloat16
HIGHEST = lax.Precision.HIGHEST

D_MODEL = 1024
D_FF = 2816
N_META = 16
N_MIXERS = 2
HG_HEADS = 8
HG_HEAD_DIM = 128
RW_HEADS = 16
RW_HEAD_DIM = 64
RW_GATE_LORA_PAD = 256
NORM_EPS = 1e-6
RW_GN_EPS = 64e-5
RW_DECAY_SCALE = 0.6065306597126334

LANES = 128
HALO = 8
CHUNK = 64
ROW_ALIGN = 256
VMEM_LIMIT_BYTES = 56 * 1024 * 1024

NT_DIMS = (((1,), (1,)), ((), ()))


def _const_spec(shape):
    nd = len(shape)
    return pl.BlockSpec(shape, lambda *_: (0,) * nd, pipeline_mode=pl.Buffered(1))


def _params(*sem):
    return pltpu.CompilerParams(dimension_semantics=sem, vmem_limit_bytes=VMEM_LIMIT_BYTES)


def _valid_rows(row0, n, seqs):
    r = row0 + lax.broadcasted_iota(jnp.int32, (n, 1), 0)
    m = None
    for start, length, _ in seqs:
        t = (r >= start) & (r < start + length)
        m = t if m is None else (m | t)
    return m


def _rms(x, g):
    return x * lax.rsqrt(jnp.mean(x * x, axis=-1, keepdims=True) + NORM_EPS) * g


def _bdot(a, b):
    return jnp.dot(a.astype(BF16), b.astype(BF16), preferred_element_type=F32)


def _bdot_nt(a, b):
    return lax.dot_general(a.astype(BF16), b.astype(BF16), NT_DIMS, preferred_element_type=F32)


RW_INV_TERMS = 1
RW_INV_BASE = 2


def _split(x, terms):
    hi = x.astype(BF16)
    if terms == 1:
        return (hi,)
    return hi, (x - hi.astype(F32)).astype(BF16)


def _mms(a, b, dims=None):
    if dims is None:
        d = lambda x, y: jnp.dot(x, y, preferred_element_type=F32)
    else:
        d = lambda x, y: lax.dot_general(x, y, dims, preferred_element_type=F32)
    out = d(a[0], b[0])
    if len(a) == 2 and len(b) == 2:
        out = out + (d(a[0], b[1]) + d(a[1], b[0]))
    return out


def _split3(x):
    x1 = x.astype(BF16)
    r1 = x - x1.astype(F32)
    x2 = r1.astype(BF16)
    x3 = (r1 - x2.astype(F32)).astype(BF16)
    return x1, x2, x3


def _dot_exact_rhs(a_bf16, x):
    x1, x2, x3 = _split3(x)
    d = lambda t: jnp.dot(a_bf16, t, preferred_element_type=F32)
    return d(x1) + d(x2) + d(x3)


def _head_sum(x, e_ref, et_ref):
    def dot2(z, ind):
        z1, z2 = _split(z, 2)
        return (jnp.dot(z1, ind, preferred_element_type=F32) + jnp.dot(z2, ind, preferred_element_type=F32))

    s = dot2(x, e_ref[...])
    return dot2(s, et_ref[...])


def _shift_rows(xe, tm):
    n = xe.shape[0]
    prev = pltpu.roll(xe, 1, axis=0)[HALO:HALO + tm]
    nxt = pltpu.roll(xe, n - 1, axis=0)[HALO:HALO + tm]
    return prev, xe[HALO:HALO + tm], nxt


def _with_halo(hp_ref, h_ref, hn_ref, n_tiles):
    i = pl.program_id(0)
    hp = jnp.where(i > 0, hp_ref[...], 0.0)
    hn = jnp.where(i < n_tiles - 1, hn_ref[...], 0.0)
    return jnp.concatenate([hp, h_ref[...], hn], axis=0)


def _halo_specs(tm, n_rows):
    r = tm // HALO
    last = n_rows // HALO - 1
    prev = pl.BlockSpec((HALO, D_MODEL), lambda i: (jnp.maximum(i * r - 1, 0), 0))
    cur = pl.BlockSpec((tm, D_MODEL), lambda i: (i, 0))
    nxt = pl.BlockSpec((HALO, D_MODEL), lambda i: (jnp.minimum((i + 1) * r, last), 0))
    return prev, cur, nxt


def _row_spec(tm, width=D_MODEL):
    return pl.BlockSpec((tm, width), lambda i: (i, 0))


def _chunk_masks():
    t = lax.broadcasted_iota(jnp.int32, (CHUNK, CHUNK), 0)
    s = lax.broadcasted_iota(jnp.int32, (CHUNK, CHUNK), 1)
    return t, s


def _hg_in_kernel(h_ref, g_ref, w_ref, lb_ref, q_ref, kf_ref, lff_ref, kb_ref, lfb_ref, i_ref, gs_ref,
                  *, seqs, tm):
    hn = _rms(h_ref[...], g_ref[...])
    p = _bdot(hn, w_ref[...])
    valid = _valid_rows(pl.program_id(0) * tm, tm, seqs).astype(F32)
    lb = lb_ref[...]
    q = p[:, 0:D_MODEL]
    q_ref[...] = q * jax.nn.sigmoid(q)
    for f, k_ref, lf_ref in ((p[:, D_MODEL:2 * D_MODEL], kf_ref, lff_ref),
                             (p[:, 2 * D_MODEL:3 * D_MODEL], kb_ref, lfb_ref)):
        lf_ref[...] = jnp.log(lb + (1.0 - lb) * jax.nn.sigmoid(f)) * valid
        k_ref[...] = (1.0 - lb) * jax.nn.sigmoid(-f) * valid
    i_ref[...] = p[:, 3 * D_MODEL:4 * D_MODEL]
    g = p[:, 4 * D_MODEL:5 * D_MODEL]
    gs_ref[...] = g * jax.nn.sigmoid(g)


def _hg_chunks(bodies, masks):
    for b in bodies:
        m = masks[b['dir']]
        cum = b['cum']
        c_ref = cum[m['mid']:m['mid'] + 1]
        c_last = cum[m['last']:m['last'] + 1]
        q, k = b['q'], b['k']
        b.update(q_in=q * jnp.exp(cum - c_ref), k_in=k * jnp.exp(c_ref - cum), q_st=q * jnp.exp(cum),
                 k_st=k * jnp.exp(c_last - cum), g=jnp.exp(c_last))
    for b in bodies:
        b['scores'] = jnp.where(masks[b['dir']]['incl'], _bdot_nt(b['q_in'], b['k_in']), 0.0)
    for b in bodies:
        b['st'] = b['st_ref'][b['idx']]
        b['o_ref'][:, b['sl']] = _bdot(b['scores'], b['iv']) + _bdot_nt(b['q_st'], b['st'])
    for b in bodies:
        b['st_ref'][b['idx']] = b['st'] * b['g'] + _bdot(b['iv'].T, b['k_st'])


def _hg_scan_kernel(qf_ref, kf_ref, lff_ref, if_ref, qb_ref, kb_ref, lfb_ref, ib_ref,
                    of_ref, ob_ref, sf_ref, sb_ref, *, fwd_starts, bwd_starts):
    c = pl.program_id(0)

    def reset_at(ref, steps):
        hit = functools.reduce(jnp.logical_or, [c == s for s in steps])

        @pl.when(hit)
        def _():
            ref[...] = jnp.zeros(ref.shape, F32)

    reset_at(sf_ref, fwd_starts)
    reset_at(sb_ref, bwd_starts)
    t, s = _chunk_masks()
    masks = {'f': dict(incl=s <= t, mid=CHUNK // 2, last=CHUNK - 1),
             'b': dict(incl=s >= t, mid=CHUNK // 2 - 1, last=0)}
    cum_f = _dot_exact_rhs((s <= t).astype(BF16), lff_ref[...])
    cum_b = _dot_exact_rhs((s >= t).astype(BF16), lfb_ref[...])
    bodies = []
    for hd in range(HG_HEADS):
        sl = slice(hd * HG_HEAD_DIM, (hd + 1) * HG_HEAD_DIM)
        bodies.append(dict(dir='f', q=qf_ref[:, sl], k=kf_ref[:, sl], iv=if_ref[:, sl], cum=cum_f[:, sl],
                           st_ref=sf_ref, idx=hd, o_ref=of_ref, sl=sl))
        bodies.append(dict(dir='b', q=qb_ref[:, sl], k=kb_ref[:, sl], iv=ib_ref[:, sl], cum=cum_b[:, sl],
                           st_ref=sb_ref, idx=hd, o_ref=ob_ref, sl=sl))
    _hg_chunks(bodies, masks)


def _hg_out_kernel(of_ref, ob_ref, gs_ref, h_ref, on_ref, w_ref, out_ref, *, seqs, tm):
    o = of_ref[...] + ob_ref[...]
    parts = []
    for hd in range(HG_HEADS):
        oh = o[:, hd * HG_HEAD_DIM:(hd + 1) * HG_HEAD_DIM]
        parts.append(oh * lax.rsqrt(jnp.mean(oh * oh, axis=-1, keepdims=True) + NORM_EPS))
    on = jnp.concatenate(parts, axis=1) * on_ref[...] * gs_ref[...]
    y = h_ref[...] + _bdot(on, w_ref[...])
    valid = _valid_rows(pl.program_id(0) * tm, tm, seqs)
    out_ref[...] = jnp.where(valid, y, 0.0)


def _hgrn2_layer(h, norm_g, w_in, w_out, onorm_g, lb, seqs, tm):
    n_rows = h.shape[0]
    n_tiles = n_rows // tm
    n_chunks = n_rows // CHUNK
    row = _row_spec(tm)
    vec = _const_spec((1, D_MODEL))
    flat = jax.ShapeDtypeStruct((n_rows, D_MODEL), F32)
    q, kf, lff, kb, lfb, iv, gs = pl.pallas_call(
        functools.partial(_hg_in_kernel, seqs=seqs, tm=tm),
        grid=(n_tiles,),
        in_specs=[row, vec, _const_spec((D_MODEL, 5 * D_MODEL)), vec],
        out_specs=[row] * 7,
        out_shape=[flat] * 7,
        compiler_params=_params("parallel"),
        name="hg_in",
    )(h, norm_g.reshape(1, -1), w_in.astype(BF16), lb.reshape(1, -1))

    fwd_starts = tuple(s // CHUNK for s, _, _ in seqs)
    bwd_starts = tuple(n_chunks - (s + p) // CHUNK for s, _, p in seqs)
    cf = pl.BlockSpec((CHUNK, D_MODEL), lambda c: (c, 0))
    cb = pl.BlockSpec((CHUNK, D_MODEL), lambda c: (n_chunks - 1 - c, 0))
    state = pltpu.VMEM((HG_HEADS, HG_HEAD_DIM, HG_HEAD_DIM), F32)
    o_f, o_b = pl.pallas_call(
        functools.partial(_hg_scan_kernel, fwd_starts=fwd_starts, bwd_starts=bwd_starts),
        grid=(n_chunks,),
        in_specs=[cf, cf, cf, cf, cb, cb, cb, cb],
        out_specs=[cf, cb],
        out_shape=[flat, flat],
        scratch_shapes=[state, state],
        compiler_params=_params("arbitrary"),
        name="hg_scan",
    )(q, kf, lff, iv, q, kb, lfb, iv)

    return pl.pallas_call(
        functools.partial(_hg_out_kernel, seqs=seqs, tm=tm),
        grid=(n_tiles,),
        in_specs=[row, row, row, row, vec, _const_spec((D_MODEL, D_MODEL))],
        out_specs=row,
        out_shape=flat,
        compiler_params=_params("parallel"),
        name="hg_out",
    )(o_f, o_b, gs, h, onorm_g.reshape(1, -1), w_out.astype(BF16))


def _rw_in_kernel(hp_ref, h_ref, hn_ref, g_ref, mu_ref, wr_ref, wk_ref, wv_ref, g1_ref, g2_ref,
                  w0_ref, w1_ref, w2_ref, a0_ref, a1_ref, a2_ref, kk_ref, ka_ref, rk_ref, e_ref, et_ref,
                  r_out, v_out, kkn_out, g_out, bonus_out,
                  lwf_out, kdf_out, abf_out, lwb_out, kdb_out, abb_out, *, seqs, tm, n_tiles):
    xe = _rms(_with_halo(hp_ref, h_ref, hn_ref, n_tiles), g_ref[...])
    xm1, x, xp1 = _shift_rows(xe, tm)
    xx = 0.5 * (xm1 + xp1) - x
    mu = mu_ref[...]
    xr, xw, xk, xv, xa, xg = (x + xx * mu[j:j + 1] for j in range(6))
    r = _bdot(xr, wr_ref[...])
    k = _bdot(xk, wk_ref[...])
    v = _bdot(xv, wv_ref[...])
    g_out[...] = _bdot(jax.nn.sigmoid(_bdot(xg, g1_ref[...])), g2_ref[...])
    z = w0_ref[...] + _bdot(jnp.tanh(_bdot(xw, w1_ref[...])), w2_ref[...])
    lw = -RW_DECAY_SCALE * jax.nn.sigmoid(z)
    a = jax.nn.sigmoid(a0_ref[...] + _bdot(_bdot(xa, a1_ref[...]), a2_ref[...]))
    kkr = k * kk_ref[...]
    ssq = _head_sum(kkr * kkr, e_ref, et_ref)
    kkn = kkr * lax.rsqrt(jnp.maximum(ssq, 1e-24))
    valid = _valid_rows(pl.program_id(0) * tm, tm, seqs).astype(F32)
    ka = ka_ref[...]
    kd_sum = None
    for d, (lw_o, kd_o, ab_o) in enumerate(((lwf_out, kdf_out, abf_out), (lwb_out, kdb_out, abb_out))):
        a_d = a[:, d * D_MODEL:(d + 1) * D_MODEL]
        kd = k * (1.0 + (a_d - 1.0) * ka)
        kd_sum = kd if kd_sum is None else kd_sum + kd
        lw_o[...] = lw[:, d * D_MODEL:(d + 1) * D_MODEL]
        kd_o[...] = kd * valid
        ab_o[...] = kkn * a_d
    r_out[...] = r
    v_out[...] = v
    kkn_out[...] = kkn
    bonus_out[...] = _head_sum(r * kd_sum * rk_ref[...], e_ref, et_ref) * v


def _rw_chunks(bodies, masks, eye, lane_a):
    stack = lambda z_: jnp.concatenate([jnp.where(lane_a, z_, 0.0), jnp.where(lane_a, 0.0, z_)], axis=0)
    sp1 = lambda z_: _split(z_, 1)
    spi = lambda z_: _split(z_, RW_INV_TERMS)
    n = 2 * CHUNK
    for b in bodies:
        m = masks[b['dir']]
        c, lw = b['c'], b['lw']
        e = c - lw
        c_mid = c[m['mid']:m['mid'] + 1]
        c_last = c[m['last']:m['last'] + 1]
        beta = -b['ab']
        dec_out = jnp.exp(c_mid - c)
        dec_end = jnp.exp(c_last - c)
        vs = stack(b['v'])
        b.update(
            lhs_a=spi(stack(b['al'] * jnp.exp(e - c_mid))), lhs_r=sp1(stack(b['r'] * jnp.exp(c - c_mid))),
            rhs=spi(jnp.concatenate([stack(beta * dec_out), stack(b['k'] * dec_out)], axis=0)),
            vs=sp1(vs), vs_t=sp1(vs.T), al_h=sp1(stack(b['al'] * jnp.exp(e))), r_h=sp1(stack(b['r'] * jnp.exp(c))),
            b_h=sp1(stack(beta * dec_end)), k_h=sp1(stack(b['k'] * dec_end)), g=jnp.exp(c_last))
    for b in bodies:
        b['sc_a'] = _mms(b['lhs_a'], b['rhs'], NT_DIMS)
    for b in bodies:
        b['sc_r'] = _mms(b['lhs_r'], b['rhs'][:1], NT_DIMS)
    for b in bodies:
        m = masks[b['dir']]
        sc_a, sc_r = b['sc_a'], b['sc_r']
        b.update(
            a_d=jnp.where(m['diag'], sc_a[:, :n], 0.0),
            a_off=[spi(jnp.where(mo, sc_a[:, :n], 0.0)) for mo in m['off']],
            a_ak=sp1(jnp.where(m['strict'], sc_a[:, n:], 0.0)),
            a_rb=sp1(jnp.where(m['incl'], sc_r[:, :n], 0.0)), a_rk=sp1(jnp.where(m['incl'], sc_r[:, n:], 0.0)))
    for b in bodies:
        b['ht'] = b['ht_ref'][b['idx']]
        b['ht_s'] = sp1(b['ht'])
        b['x'] = _mms(b['al_h'], b['ht_s'], NT_DIMS) + _mms(b['a_ak'], b['vs'])
    for b in bodies:
        b['pw'] = spi(b['a_d'])
        b['inv'] = eye + b['a_d']
    for _ in range(RW_INV_BASE.bit_length() - 2):
        for b in bodies:
            b['pw'] = spi(_mms(b['pw'], b['pw']))
        for b in bodies:
            b['inv'] = b['inv'] + _mms(spi(b['inv']), b['pw'])
    for level in range(len(masks['f']['off'])):
        for b in bodies:
            b['inv_s'] = spi(b['inv'])
            b['t'] = _mms(b['a_off'][level], b['inv_s'])
        for b in bodies:
            b['inv'] = b['inv'] + _mms(b['inv_s'], spi(b['t']))
    for b in bodies:
        b['u'] = _mms(sp1(b['inv']), sp1(b['x']))
    for b in bodies:
        y = _mms(b['r_h'], b['ht_s'], NT_DIMS) + _mms(b['a_rb'], sp1(b['u'])) + _mms(b['a_rk'], b['vs'])
        b['y_ref'][:, b['sl']] = y[:CHUNK] + y[CHUNK:]
    for b in bodies:
        b['ht_ref'][b['idx']] = b['ht'] * b['g'] + _mms(sp1(b['u'].T), b['b_h']) + _mms(b['vs_t'], b['k_h'])


def _rw_scan_kernel(rf_ref, kf_ref, vf_ref, alf_ref, abf_ref, lwf_ref,
                    rb_ref, kb_ref, vb_ref, alb_ref, abb_ref, lwb_ref,
                    yf_ref, yb_ref, hf_ref, hb_ref, *, fwd_starts, bwd_starts, pairs):
    c = pl.program_id(1)

    def reset_at(ref, steps):
        hit = functools.reduce(jnp.logical_or, [c == s for s in steps])

        @pl.when(hit)
        def _():
            ref[...] = jnp.zeros(ref.shape, F32)

    reset_at(hf_ref, fwd_starts)
    reset_at(hb_ref, bwd_starts)
    t, s = _chunk_masks()
    n = 2 * CHUNK
    ti = lax.broadcasted_iota(jnp.int32, (n, n), 0)
    si = lax.broadcasted_iota(jnp.int32, (n, n), 1)
    same = lambda width: (ti // width) == (si // width)
    eye = (ti == si).astype(F32)
    lane_a = lax.broadcasted_iota(jnp.int32, (CHUNK, LANES), 1) < RW_HEAD_DIM

    widths = [RW_INV_BASE << j for j in range((CHUNK // RW_INV_BASE).bit_length() - 1)]

    def dir_masks(before, before_incl, mid, last):
        return dict(mid=mid, last=last,
                    strict=same(CHUNK) & before, incl=same(CHUNK) & before_incl,
                    diag=same(RW_INV_BASE) & before,
                    off=[same(2 * w) & ~same(w) & before for w in widths])

    masks = {'f': dir_masks(si < ti, si <= ti, CHUNK // 2, CHUNK - 1),
             'b': dir_masks(si > ti, si >= ti, CHUNK // 2 - 1, 0)}
    cum_f = _dot_exact_rhs((s <= t).astype(BF16), lwf_ref[...])
    cum_b = _dot_exact_rhs((s >= t).astype(BF16), lwb_ref[...])
    bodies = []
    for p in range(pairs):
        sl = slice(p * LANES, (p + 1) * LANES)
        bodies.append(dict(dir='f', r=rf_ref[:, sl], k=kf_ref[:, sl], v=vf_ref[:, sl], al=alf_ref[:, sl],
                           ab=abf_ref[:, sl], lw=lwf_ref[:, sl], c=cum_f[:, sl], ht_ref=hf_ref, idx=p,
                           y_ref=yf_ref, sl=sl))
        bodies.append(dict(dir='b', r=rb_ref[:, sl], k=kb_ref[:, sl], v=vb_ref[:, sl], al=alb_ref[:, sl],
                           ab=abb_ref[:, sl], lw=lwb_ref[:, sl], c=cum_b[:, sl], ht_ref=hb_ref, idx=p,
                           y_ref=yb_ref, sl=sl))
    _rw_chunks(bodies, masks, eye, lane_a)


def _rw_out_kernel(yf_ref, yb_ref, bonus_ref, g_ref, h_ref, gnw_ref, gnb_ref, e_ref, et_ref, w_ref, out_ref,
                   *, seqs, tm):
    y = yf_ref[...] + yb_ref[...]
    inv_n = 1.0 / RW_HEAD_DIM
    mean = _head_sum(y, e_ref, et_ref) * inv_n
    dlt = y - mean
    var = _head_sum(dlt * dlt, e_ref, et_ref) * inv_n
    yn = dlt * lax.rsqrt(var + RW_GN_EPS) * gnw_ref[...] + gnb_ref[...]
    out = (yn + bonus_ref[...]) * g_ref[...]
    res = h_ref[...] + _bdot(out, w_ref[...])
    valid = _valid_rows(pl.program_id(0) * tm, tm, seqs)
    out_ref[...] = jnp.where(valid, res, 0.0)


def _rwkv7_layer(h, norm_g, mu, w_r, w_k, w_v, w_o, w0, w1, w2, a0, a1, a2, g1, g2, k_k, k_a, r_k, gn_w, gn_b,
                 seqs, tm_in, tm_out, pairs):
    n_rows = h.shape[0]
    n_chunks = n_rows // CHUNK
    flat = jax.ShapeDtypeStruct((n_rows, D_MODEL), F32)
    vec = _const_spec((1, D_MODEL))
    vec2 = _const_spec((1, 2 * D_MODEL))
    row1 = lambda t: t.reshape(1, -1)

    def cat_dirs(w):
        return jnp.concatenate([w[0], w[1]], axis=1).astype(BF16)

    def blockdiag_dirs(w):
        z = jnp.zeros_like(w[0])
        return jnp.concatenate([jnp.concatenate([w[0], z], axis=1),
                                jnp.concatenate([z, w[1]], axis=1)], axis=0).astype(BF16)

    lora = g1.shape[1]
    g1p = jnp.pad(g1, ((0, 0), (0, RW_GATE_LORA_PAD - lora))).astype(BF16)
    g2p = jnp.pad(g2, ((0, RW_GATE_LORA_PAD - lora), (0, 0))).astype(BF16)
    w1c, w2c, a1c, a2c = cat_dirs(w1), blockdiag_dirs(w2), cat_dirs(a1), blockdiag_dirs(a2)
    e = (lax.broadcasted_iota(jnp.int32, (D_MODEL, LANES), 0) // RW_HEAD_DIM
         == lax.broadcasted_iota(jnp.int32, (D_MODEL, LANES), 1)).astype(BF16)
    et = e.T

    n_tiles = n_rows // tm_in
    prev, cur, nxt = _halo_specs(tm_in, n_rows)
    row = _row_spec(tm_in)
    sq = _const_spec((D_MODEL, D_MODEL))
    outs = pl.pallas_call(
        functools.partial(_rw_in_kernel, seqs=seqs, tm=tm_in, n_tiles=n_tiles),
        grid=(n_tiles,),
        in_specs=[prev, cur, nxt, vec, _const_spec((6, D_MODEL)), sq, sq, sq,
                  _const_spec(g1p.shape), _const_spec(g2p.shape),
                  vec2, _const_spec(w1c.shape), _const_spec(w2c.shape),
                  vec2, _const_spec(a1c.shape), _const_spec(a2c.shape),
                  vec, vec, vec, _const_spec(e.shape), _const_spec(et.shape)],
        out_specs=[row] * 11,
        out_shape=[flat] * 11,
        compiler_params=_params("parallel"),
        name="rw_in",
    )(h, h, h, row1(norm_g), mu, w_r.astype(BF16), w_k.astype(BF16), w_v.astype(BF16), g1p, g2p,
      row1(w0), w1c, w2c, row1(a0), a1c, a2c, row1(k_k), row1(k_a), row1(r_k), e, et)
    r, v, kkn, g, bonus, lwf, kdf, abf, lwb, kdb, abb = outs

    width = pairs * LANES
    n_groups = D_MODEL // width
    fwd_starts = tuple(s // CHUNK for s, _, _ in seqs)
    bwd_starts = tuple(n_chunks - (s + p) // CHUNK for s, _, p in seqs)
    cf = pl.BlockSpec((CHUNK, width), lambda p, c: (c, p))
    cb = pl.BlockSpec((CHUNK, width), lambda p, c: (n_chunks - 1 - c, p))
    state = pltpu.VMEM((pairs, LANES, LANES), F32)
    y_f, y_b = pl.pallas_call(
        functools.partial(_rw_scan_kernel, fwd_starts=fwd_starts, bwd_starts=bwd_starts, pairs=pairs),
        grid=(n_groups, n_chunks),
        in_specs=[cf] * 6 + [cb] * 6,
        out_specs=[cf, cb],
        out_shape=[flat, flat],
        scratch_shapes=[state, state],
        compiler_params=_params("parallel", "arbitrary"),
        name="rw_scan",
    )(r, kdf, v, kkn, abf, lwf, r, kdb, v, kkn, abb, lwb)

    row = _row_spec(tm_out)
    return pl.pallas_call(
        functools.partial(_rw_out_kernel, seqs=seqs, tm=tm_out),
        grid=(n_rows // tm_out,),
        in_specs=[row, row, row, row, row, vec, vec, _const_spec(e.shape), _const_spec(et.shape), sq],
        out_specs=row,
        out_shape=flat,
        compiler_params=_params("parallel"),
        name="rw_out",
    )(y_f, y_b, bonus, g, h, row1(gn_w), row1(gn_b), e, et, w_o.astype(BF16))


def _ffn_kernel(hp_ref, h_ref, hn_ref, g_ref, wu_ref, wv_ref, cw_ref, cb_ref, wo_ref, gf_ref, out_ref,
                *, seqs, tm, n_tiles, final):
    he = _with_halo(hp_ref, h_ref, hn_ref, n_tiles)
    xe = _rms(he, g_ref[...])
    ue = _bdot(xe, wu_ref[...])
    v = _bdot(xe[HALO:HALO + tm], wv_ref[...])
    um1, u0, up1 = _shift_rows(ue, tm)
    cw = cw_ref[...]
    u = um1 * cw[0:1] + u0 * cw[1:2] + up1 * cw[2:3] + cb_ref[...]
    y = he[HALO:HALO + tm] + _bdot(jax.nn.gelu(u) * v, wo_ref[...])
    valid = _valid_rows(pl.program_id(0) * tm, tm, seqs)
    y = jnp.where(valid, y, 0.0)
    if final:
        y = _rms(y, gf_ref[...])
    out_ref[...] = y


def _ffn_layer(h, norm_g, w_in, conv_w, conv_b, w_out, final_g, seqs, tm, final):
    n_rows = h.shape[0]
    n_tiles = n_rows // tm
    prev, cur, nxt = _halo_specs(tm, n_rows)
    vec = _const_spec((1, D_MODEL))
    vecf = _const_spec((1, D_FF))
    return pl.pallas_call(
        functools.partial(_ffn_kernel, seqs=seqs, tm=tm, n_tiles=n_tiles, final=final),
        grid=(n_tiles,),
        in_specs=[prev, cur, nxt, vec, _const_spec((D_MODEL, D_FF)), _const_spec((D_MODEL, D_FF)),
                  _const_spec((3, D_FF)), vecf, _const_spec((D_FF, D_MODEL)), vec],
        out_specs=_row_spec(tm),
        out_shape=jax.ShapeDtypeStruct((n_rows, D_MODEL), F32),
        compiler_params=_params("parallel"),
        name="ffn",
    )(h, h, h, norm_g.reshape(1, -1), w_in[:, :D_FF].astype(BF16), w_in[:, D_FF:].astype(BF16),
      conv_w, conv_b.reshape(1, -1), w_out.astype(BF16), final_g.reshape(1, -1))


def _layout(lengths):
    seqs, start = [], 0
    for n in lengths:
        padded = -(-n // ROW_ALIGN) * ROW_ALIGN
        seqs.append((start, n, padded))
        start += padded
    return tuple(seqs), start


def _trunk(xs, p, tm=256, tm_rw_in=256, rw_pairs=8):
    seqs, n_rows = _layout([N_META + x.shape[0] for x in xs])
    meta = p['meta_tokens'].astype(F32)
    pieces = []
    for x, (_, n, padded) in zip(xs, seqs):
        pieces += [meta, x, jnp.zeros((padded - n, D_MODEL), F32)]
    h = jnp.concatenate(pieces, axis=0)
    lb_all = jnp.cumsum(jax.nn.softmax(p['hg_lower_bound'].astype(F32), axis=0), axis=0)
    depth = p['norm_mix'].shape[0]
    for layer in range(depth):
        j = layer // N_MIXERS
        if layer % N_MIXERS == 0:
            h = _hgrn2_layer(h, p['norm_mix'][layer], p['hg_w_in'][j], p['hg_w_out'][j], p['hg_onorm'][j],
                             lb_all[j], seqs, tm)
        else:
            h = _rwkv7_layer(h, p['norm_mix'][layer], p['rw_mu'][j], p['rw_w_r'][j], p['rw_w_k'][j],
                             p['rw_w_v'][j], p['rw_w_o'][j], p['rw_w0'][j], p['rw_w1'][j], p['rw_w2'][j],
                             p['rw_a0'][j], p['rw_a1'][j], p['rw_a2'][j], p['rw_g1'][j], p['rw_g2'][j],
                             p['rw_k_k'][j], p['rw_k_a'][j], p['rw_r_k'][j], p['rw_gn_w'][j], p['rw_gn_b'][j],
                             seqs, tm_rw_in, tm, rw_pairs)
        h = _ffn_layer(h, p['norm_ffn'][layer], p['ffn_w_in'][layer], p['ffn_conv_w'][layer],
                       p['ffn_conv_b'][layer], p['ffn_w_out'][layer], p['norm_final'], seqs, tm,
                       final=(layer == depth - 1))
    return [h[s + N_META:s + n] for s, n, _ in seqs]


def kernel(x_prompt, x_sample, meta_tokens, norm_mix, norm_ffn, norm_final, hg_w_in, hg_w_out, hg_lower_bound,
           hg_onorm, rw_mu, rw_w_r, rw_w_k, rw_w_v, rw_w_o, rw_w0, rw_w1, rw_w2, rw_a0, rw_a1, rw_a2, rw_g1,
           rw_g2, rw_k_k, rw_k_a, rw_r_k, rw_gn_w, rw_gn_b, ffn_w_in, ffn_conv_w, ffn_conv_b, ffn_w_out):
    p = {
        'meta_tokens': meta_tokens, 'norm_mix': norm_mix, 'norm_ffn': norm_ffn, 'norm_final': norm_final,
        'hg_w_in': hg_w_in, 'hg_w_out': hg_w_out, 'hg_lower_bound': hg_lower_bound, 'hg_onorm': hg_onorm,
        'rw_mu': rw_mu, 'rw_w_r': rw_w_r, 'rw_w_k': rw_w_k, 'rw_w_v': rw_w_v, 'rw_w_o': rw_w_o,
        'rw_w0': rw_w0, 'rw_w1': rw_w1, 'rw_w2': rw_w2, 'rw_a0': rw_a0, 'rw_a1': rw_a1, 'rw_a2': rw_a2,
        'rw_g1': rw_g1, 'rw_g2': rw_g2, 'rw_k_k': rw_k_k, 'rw_k_a': rw_k_a, 'rw_r_k': rw_r_k,
        'rw_gn_w': rw_gn_w, 'rw_gn_b': rw_gn_b,
        'ffn_w_in': ffn_w_in, 'ffn_conv_w': ffn_conv_w, 'ffn_conv_b': ffn_conv_b, 'ffn_w_out': ffn_w_out,
    }
    xs = [x_prompt[b] for b in range(x_prompt.shape[0])] + [x_sample[b] for b in range(x_sample.shape[0])]
    ys = _trunk(xs, p)
    n_p = x_prompt.shape[0]
    return (jnp.stack(ys[:n_p], axis=0), jnp.stack(ys[n_p:], axis=0))
```

```python
import functools

import jax
import jax.numpy as jnp
from jax import lax
from jax.experimental import pallas as pl
from jax.experimental.pallas import tpu as pltpu

F32 = jnp.float32
BF16 = jnp.bfloat16
HIGHEST = lax.Precision.HIGHEST

D_MODEL = 1024
D_FF = 2816
N_META = 16
N_MIXERS = 2
HG_HEADS = 8
HG_HEAD_DIM = 128
RW_HEADS = 16
RW_HEAD_DIM = 64
RW_GATE_LORA_PAD = 256
NORM_EPS = 1e-6
RW_GN_EPS = 64e-5
RW_DECAY_SCALE = 0.6065306597126334

LANES = 128
HALO = 8
CHUNK = 64
HG_CHUNKS_PER_STEP = 4
RW_CHUNKS_PER_STEP = 2
ROW_ALIGN = 256
VMEM_LIMIT_BYTES = 56 * 1024 * 1024

NT_DIMS = (((1,), (1,)), ((), ()))


def _const_spec(shape):
    nd = len(shape)
    return pl.BlockSpec(shape, lambda *_: (0,) * nd, pipeline_mode=pl.Buffered(1))


def _params(*sem):
    return pltpu.CompilerParams(dimension_semantics=sem, vmem_limit_bytes=VMEM_LIMIT_BYTES)


def _valid_rows(row0, n, seqs):
    r = row0 + lax.broadcasted_iota(jnp.int32, (n, 1), 0)
    m = None
    for start, length, _ in seqs:
        t = (r >= start) & (r < start + length)
        m = t if m is None else (m | t)
    return m


def _rms(x, g):
    return x * lax.rsqrt(jnp.mean(x * x, axis=-1, keepdims=True) + NORM_EPS) * g


def _bdot(a, b):
    return jnp.dot(a.astype(BF16), b.astype(BF16), preferred_element_type=F32)


def _bdot_nt(a, b):
    return lax.dot_general(a.astype(BF16), b.astype(BF16), NT_DIMS, preferred_element_type=F32)


RW_INV_TERMS = 1
RW_INV_BASE = 4


def _split(x, terms):
    hi = x.astype(BF16)
    if terms == 1:
        return (hi,)
    return hi, (x - hi.astype(F32)).astype(BF16)


def _mms(a, b, dims=None):
    if dims is None:
        d = lambda x, y: jnp.dot(x, y, preferred_element_type=F32)
    else:
        d = lambda x, y: lax.dot_general(x, y, dims, preferred_element_type=F32)
    out = d(a[0], b[0])
    if len(a) == 2 and len(b) == 2:
        out = out + (d(a[0], b[1]) + d(a[1], b[0]))
    return out


def _split3(x):
    x1 = x.astype(BF16)
    r1 = x - x1.astype(F32)
    x2 = r1.astype(BF16)
    x3 = (r1 - x2.astype(F32)).astype(BF16)
    return x1, x2, x3


def _dot_exact_rhs(a_bf16, x):
    x1, x2, x3 = _split3(x)
    d = lambda t: jnp.dot(a_bf16, t, preferred_element_type=F32)
    return d(x1) + d(x2) + d(x3)


def _head_sum(x, e_ref, et_ref):
    def dot2(z, ind):
        z1, z2 = _split(z, 2)
        return (jnp.dot(z1, ind, preferred_element_type=F32) + jnp.dot(z2, ind, preferred_element_type=F32))

    s = dot2(x, e_ref[...])
    return dot2(s, et_ref[...])


def _shift_rows(xe, tm):
    n = xe.shape[0]
    prev = pltpu.roll(xe, 1, axis=0)[HALO:HALO + tm]
    nxt = pltpu.roll(xe, n - 1, axis=0)[HALO:HALO + tm]
    return prev, xe[HALO:HALO + tm], nxt


def _with_halo(hp_ref, h_ref, hn_ref, n_tiles):
    i = pl.program_id(0)
    hp = jnp.where(i > 0, hp_ref[...], 0.0)
    hn = jnp.where(i < n_tiles - 1, hn_ref[...], 0.0)
    return jnp.concatenate([hp, h_ref[...], hn], axis=0)


def _halo_specs(tm, n_rows):
    r = tm // HALO
    last = n_rows // HALO - 1
    prev = pl.BlockSpec((HALO, D_MODEL), lambda i: (jnp.maximum(i * r - 1, 0), 0))
    cur = pl.BlockSpec((tm, D_MODEL), lambda i: (i, 0))
    nxt = pl.BlockSpec((HALO, D_MODEL), lambda i: (jnp.minimum((i + 1) * r, last), 0))
    return prev, cur, nxt


def _row_spec(tm, width=D_MODEL):
    return pl.BlockSpec((tm, width), lambda i: (i, 0))


def _chunk_masks():
    t = lax.broadcasted_iota(jnp.int32, (CHUNK, CHUNK), 0)
    s = lax.broadcasted_iota(jnp.int32, (CHUNK, CHUNK), 1)
    return t, s


def _hg_in_kernel(h_ref, g_ref, w_ref, lb_ref, q_ref, kf_ref, lff_ref, kb_ref, lfb_ref, i_ref, gs_ref,
                  *, seqs, tm):
    hn = _rms(h_ref[...], g_ref[...]).astype(BF16)
    proj = lambda j: jnp.dot(hn, w_ref[:, j * D_MODEL:(j + 1) * D_MODEL], preferred_element_type=F32)
    valid = _valid_rows(pl.program_id(0) * tm, tm, seqs).astype(F32)
    lb = lb_ref[...]
    q = proj(0)
    q_ref[...] = q * jax.nn.sigmoid(q)
    for j, k_ref, lf_ref in ((1, kf_ref, lff_ref), (2, kb_ref, lfb_ref)):
        sg = jax.nn.sigmoid(proj(j))
        lf_ref[...] = jnp.log(lb + (1.0 - lb) * sg) * valid
        k_ref[...] = (1.0 - lb) * (1.0 - sg) * valid
    i_ref[...] = proj(3)
    g = proj(4)
    gs_ref[...] = g * jax.nn.sigmoid(g)


def _hg_chunks(rounds, masks):
    for bodies in rounds:
        for b in bodies:
            m = masks[b['dir']]
            cum = b['cum']
            c_ref = cum[m['mid']:m['mid'] + 1]
            c_last = cum[m['last']:m['last'] + 1]
            q, k = b['q'], b['k']
            b.update(q_in=q * jnp.exp(cum - c_ref), k_in=k * jnp.exp(c_ref - cum), q_st=q * jnp.exp(cum),
                     k_st=k * jnp.exp(c_last - cum), g=jnp.exp(c_last))
        for b in bodies:
            b['scores'] = jnp.where(masks[b['dir']]['incl'], _bdot_nt(b['q_in'], b['k_in']), 0.0)
    for bodies in rounds:
        for b in bodies:
            b['st'] = b['st_ref'][b['idx']]
            b['o_ref'][b['rows'], b['sl']] = _bdot(b['scores'], b['iv']) + _bdot_nt(b['q_st'], b['st'])
        for b in bodies:
            b['st_ref'][b['idx']] = b['st'] * b['g'] + _bdot(b['iv'].T, b['k_st'])


def _hg_scan_kernel(qf_ref, kf_ref, lff_ref, if_ref, qb_ref, kb_ref, lfb_ref, ib_ref,
                    of_ref, ob_ref, sf_ref, sb_ref, *, fwd_starts, bwd_starts, chunks):
    c = pl.program_id(0)

    def reset_at(ref, steps):
        hit = functools.reduce(jnp.logical_or, [c == s for s in steps])

        @pl.when(hit)
        def _():
            ref[...] = jnp.zeros(ref.shape, F32)

    reset_at(sf_ref, fwd_starts)
    reset_at(sb_ref, bwd_starts)
    t, s = _chunk_masks()
    masks = {'f': dict(incl=s <= t, mid=CHUNK // 2, last=CHUNK - 1),
             'b': dict(incl=s >= t, mid=CHUNK // 2 - 1, last=0)}
    tri_f, tri_b = (s <= t).astype(BF16), (s >= t).astype(BF16)
    rounds = []
    for j in range(chunks):
        rows_f = slice(j * CHUNK, (j + 1) * CHUNK)
        rows_b = slice((chunks - 1 - j) * CHUNK, (chunks - j) * CHUNK)
        cum_f = _dot_exact_rhs(tri_f, lff_ref[rows_f, :])
        cum_b = _dot_exact_rhs(tri_b, lfb_ref[rows_b, :])
        bodies = []
        for hd in range(HG_HEADS):
            sl = slice(hd * HG_HEAD_DIM, (hd + 1) * HG_HEAD_DIM)
            bodies.append(dict(dir='f', q=qf_ref[rows_f, sl], k=kf_ref[rows_f, sl], iv=if_ref[rows_f, sl],
                               cum=cum_f[:, sl], st_ref=sf_ref, idx=hd, o_ref=of_ref, rows=rows_f, sl=sl))
            bodies.append(dict(dir='b', q=qb_ref[rows_b, sl], k=kb_ref[rows_b, sl], iv=ib_ref[rows_b, sl],
                               cum=cum_b[:, sl], st_ref=sb_ref, idx=hd, o_ref=ob_ref, rows=rows_b, sl=sl))
        rounds.append(bodies)
    _hg_chunks(rounds, masks)


def _hg_out_kernel(of_ref, ob_ref, gs_ref, h_ref, on_ref, w_ref, out_ref, *, seqs, tm):
    o = of_ref[...] + ob_ref[...]
    parts = []
    for hd in range(HG_HEADS):
        oh = o[:, hd * HG_HEAD_DIM:(hd + 1) * HG_HEAD_DIM]
        parts.append(oh * lax.rsqrt(jnp.mean(oh * oh, axis=-1, keepdims=True) + NORM_EPS))
    on = jnp.concatenate(parts, axis=1) * on_ref[...] * gs_ref[...]
    y = h_ref[...] + _bdot(on, w_ref[...])
    valid = _valid_rows(pl.program_id(0) * tm, tm, seqs)
    out_ref[...] = jnp.where(valid, y, 0.0)


def _hgrn2_layer(h, norm_g, w_in, w_out, onorm_g, lb, seqs, tm):
    n_rows = h.shape[0]
    n_tiles = n_rows // tm
    n_chunks = n_rows // CHUNK
    row = _row_spec(tm)
    vec = _const_spec((1, D_MODEL))
    flat = jax.ShapeDtypeStruct((n_rows, D_MODEL), F32)
    q, kf, lff, kb, lfb, iv, gs = pl.pallas_call(
        functools.partial(_hg_in_kernel, seqs=seqs, tm=tm),
        grid=(n_tiles,),
        in_specs=[row, vec, _const_spec((D_MODEL, 5 * D_MODEL)), vec],
        out_specs=[row] * 7,
        out_shape=[flat] * 7,
        compiler_params=_params("parallel"),
        name="hg_in",
    )(h, norm_g.reshape(1, -1), w_in.astype(BF16), lb.reshape(1, -1))

    span = HG_CHUNKS_PER_STEP * CHUNK
    n_steps = n_rows // span
    fwd_starts = tuple(s // span for s, _, _ in seqs)
    bwd_starts = tuple(n_steps - (s + p) // span for s, _, p in seqs)
    cf = pl.BlockSpec((span, D_MODEL), lambda c: (c, 0))
    cb = pl.BlockSpec((span, D_MODEL), lambda c: (n_steps - 1 - c, 0))
    state = pltpu.VMEM((HG_HEADS, HG_HEAD_DIM, HG_HEAD_DIM), F32)
    o_f, o_b = pl.pallas_call(
        functools.partial(_hg_scan_kernel, fwd_starts=fwd_starts, bwd_starts=bwd_starts,
                          chunks=HG_CHUNKS_PER_STEP),
        grid=(n_steps,),
        in_specs=[cf, cf, cf, cf, cb, cb, cb, cb],
        out_specs=[cf, cb],
        out_shape=[flat, flat],
        scratch_shapes=[state, state],
        compiler_params=_params("arbitrary"),
        name="hg_scan",
    )(q, kf, lff, iv, q, kb, lfb, iv)

    return pl.pallas_call(
        functools.partial(_hg_out_kernel, seqs=seqs, tm=tm),
        grid=(n_tiles,),
        in_specs=[row, row, row, row, vec, _const_spec((D_MODEL, D_MODEL))],
        out_specs=row,
        out_shape=flat,
        compiler_params=_params("parallel"),
        name="hg_out",
    )(o_f, o_b, gs, h, onorm_g.reshape(1, -1), w_out.astype(BF16))


def _rw_in_kernel(hp_ref, h_ref, hn_ref, g_ref, mu_ref, wr_ref, wk_ref, wv_ref, g1_ref, g2_ref,
                  w0_ref, w1_ref, w2_ref, a0_ref, a1_ref, a2_ref, kk_ref, ka_ref, rk_ref, e_ref, et_ref,
                  r_out, v_out, kkn_out, g_out, bonus_out,
                  lwf_out, kdf_out, abf_out, lwb_out, kdb_out, abb_out, *, seqs, tm, n_tiles):
    xe = _rms(_with_halo(hp_ref, h_ref, hn_ref, n_tiles), g_ref[...])
    xm1, x, xp1 = _shift_rows(xe, tm)
    xx = 0.5 * (xm1 + xp1) - x
    mu = mu_ref[...]
    xr, xw, xk, xv, xa, xg = (x + xx * mu[j:j + 1] for j in range(6))
    r = _bdot(xr, wr_ref[...])
    k = _bdot(xk, wk_ref[...])
    v = _bdot(xv, wv_ref[...])
    g_out[...] = _bdot(jax.nn.sigmoid(_bdot(xg, g1_ref[...])), g2_ref[...])
    z = w0_ref[...] + _bdot(jnp.tanh(_bdot(xw, w1_ref[...])), w2_ref[...])
    lw = -RW_DECAY_SCALE * jax.nn.sigmoid(z)
    a = jax.nn.sigmoid(a0_ref[...] + _bdot(_bdot(xa, a1_ref[...]), a2_ref[...]))
    kkr = k * kk_ref[...]
    ssq = _head_sum(kkr * kkr, e_ref, et_ref)
    kkn = kkr * lax.rsqrt(jnp.maximum(ssq, 1e-24))
    valid = _valid_rows(pl.program_id(0) * tm, tm, seqs).astype(F32)
    ka = ka_ref[...]
    kd_sum = None
    for d, (lw_o, kd_o, ab_o) in enumerate(((lwf_out, kdf_out, abf_out), (lwb_out, kdb_out, abb_out))):
        a_d = a[:, d * D_MODEL:(d + 1) * D_MODEL]
        kd = k * (1.0 + (a_d - 1.0) * ka)
        kd_sum = kd if kd_sum is None else kd_sum + kd
        lw_o[...] = lw[:, d * D_MODEL:(d + 1) * D_MODEL]
        kd_o[...] = kd * valid
        ab_o[...] = kkn * a_d
    r_out[...] = r
    v_out[...] = v
    kkn_out[...] = kkn
    bonus_out[...] = _head_sum(r * kd_sum * rk_ref[...], e_ref, et_ref) * v


def _rw_prepare(bodies, masks, eye, lane_a):
    stack = lambda z_: jnp.concatenate([jnp.where(lane_a, z_, 0.0), jnp.where(lane_a, 0.0, z_)], axis=0)
    sp1 = lambda z_: _split(z_, 1)
    spi = lambda z_: _split(z_, RW_INV_TERMS)
    n = 2 * CHUNK
    for b in bodies:
        m = masks[b['dir']]
        c, lw = b['c'], b['lw']
        e = c - lw
        c_mid = c[m['mid']:m['mid'] + 1]
        c_last = c[m['last']:m['last'] + 1]
        beta = -b['ab']
        dec_out = jnp.exp(c_mid - c)
        dec_end = jnp.exp(c_last - c)
        vs = stack(b['v'])
        b.update(
            lhs_a=spi(stack(b['al'] * jnp.exp(e - c_mid))), lhs_r=sp1(stack(b['r'] * jnp.exp(c - c_mid))),
            rhs=spi(jnp.concatenate([beta * dec_out] * 2 + [b['k'] * dec_out] * 2, axis=0)),
            vs=sp1(vs), vs_t=sp1(vs.T), al_h=sp1(stack(b['al'] * jnp.exp(e))), r_h=sp1(stack(b['r'] * jnp.exp(c))),
            b_h=sp1(stack(beta * dec_end)), k_h=sp1(stack(b['k'] * dec_end)), g=jnp.exp(c_last))
    for b in bodies:
        b['sc_a'] = _mms(b['lhs_a'], b['rhs'], NT_DIMS)
    for b in bodies:
        b['sc_r'] = _mms(b['lhs_r'], b['rhs'][:1], NT_DIMS)
    for b in bodies:
        m = masks[b['dir']]
        sc_a, sc_r = b['sc_a'], b['sc_r']
        b.update(
            a_d=jnp.where(m['diag'], sc_a[:, :n], 0.0),
            a_off=[spi(jnp.where(mo, sc_a[:, :n], 0.0)) for mo in m['off']],
            a_ak=sp1(jnp.where(m['strict'], sc_a[:, n:], 0.0)),
            a_rb=sp1(jnp.where(m['incl'], sc_r[:, :n], 0.0)), a_rk=sp1(jnp.where(m['incl'], sc_r[:, n:], 0.0)))
    for b in bodies:
        b['pw'] = spi(b['a_d'])
        b['inv'] = eye + b['a_d']
    for _ in range(RW_INV_BASE.bit_length() - 2):
        for b in bodies:
            b['pw'] = spi(_mms(b['pw'], b['pw']))
        for b in bodies:
            b['inv'] = b['inv'] + _mms(spi(b['inv']), b['pw'])
    for level in range(len(masks['f']['off'])):
        for b in bodies:
            b['inv_s'] = spi(b['inv'])
            b['t'] = _mms(b['a_off'][level], b['inv_s'])
        for b in bodies:
            b['inv'] = b['inv'] + _mms(b['inv_s'], spi(b['t']))
    for b in bodies:
        b['inv16'] = b['inv'].astype(BF16)


def _rw_advance(bodies):
    for b in bodies:
        b['ht'] = b['ht_ref'][b['idx']]
        b['h16'] = b['ht'].T.astype(BF16)
        b['x'] = jnp.dot(jnp.concatenate([b['al_h'][0], b['a_ak'][0]], axis=1),
                         jnp.concatenate([b['h16'], b['vs'][0]], axis=0), preferred_element_type=F32)
    for b in bodies:
        b['u'] = jnp.dot(b['inv16'], b['x'].astype(BF16), preferred_element_type=F32)
    for b in bodies:
        y = jnp.dot(jnp.concatenate([b['r_h'][0], b['a_rb'][0], b['a_rk'][0]], axis=1),
                    jnp.concatenate([b['h16'], b['u'].astype(BF16), b['vs'][0]], axis=0),
                    preferred_element_type=F32)
        b['y_ref'][b['rows'], b['sl']] = y[:CHUNK] + y[CHUNK:]
    for b in bodies:
        b['ht_ref'][b['idx']] = b['ht'] * b['g'] + jnp.dot(
            jnp.concatenate([b['u'].T.astype(BF16), b['vs_t'][0]], axis=1),
            jnp.concatenate([b['b_h'][0], b['k_h'][0]], axis=0), preferred_element_type=F32)


def _rw_scan_kernel(rf_ref, kf_ref, vf_ref, alf_ref, abf_ref, lwf_ref,
                    rb_ref, kb_ref, vb_ref, alb_ref, abb_ref, lwb_ref,
                    yf_ref, yb_ref, hf_ref, hb_ref, *, fwd_starts, bwd_starts, pairs, chunks):
    c = pl.program_id(1)

    def reset_at(ref, steps):
        hit = functools.reduce(jnp.logical_or, [c == s for s in steps])

        @pl.when(hit)
        def _():
            ref[...] = jnp.zeros(ref.shape, F32)

    reset_at(hf_ref, fwd_starts)
    reset_at(hb_ref, bwd_starts)
    t, s = _chunk_masks()
    n = 2 * CHUNK
    ti = lax.broadcasted_iota(jnp.int32, (n, n), 0)
    si = lax.broadcasted_iota(jnp.int32, (n, n), 1)
    same = lambda width: (ti // width) == (si // width)
    eye = (ti == si).astype(F32)
    lane_a = lax.broadcasted_iota(jnp.int32, (CHUNK, LANES), 1) < RW_HEAD_DIM

    widths = [RW_INV_BASE << j for j in range((CHUNK // RW_INV_BASE).bit_length() - 1)]

    def dir_masks(before, before_incl, mid, last):
        return dict(mid=mid, last=last,
                    strict=same(CHUNK) & before, incl=same(CHUNK) & before_incl,
                    diag=same(RW_INV_BASE) & before,
                    off=[same(2 * w) & ~same(w) & before for w in widths])

    masks = {'f': dir_masks(si < ti, si <= ti, CHUNK // 2, CHUNK - 1),
             'b': dir_masks(si > ti, si >= ti, CHUNK // 2 - 1, 0)}
    tri_f, tri_b = (s <= t).astype(BF16), (s >= t).astype(BF16)
    rounds = []
    for j in range(chunks):
        rf = slice(j * CHUNK, (j + 1) * CHUNK)
        rb = slice((chunks - 1 - j) * CHUNK, (chunks - j) * CHUNK)
        cum_f = _dot_exact_rhs(tri_f, lwf_ref[rf, :])
        cum_b = _dot_exact_rhs(tri_b, lwb_ref[rb, :])
        bodies = []
        for p in range(pairs):
            sl = slice(p * LANES, (p + 1) * LANES)
            bodies.append(dict(dir='f', r=rf_ref[rf, sl], k=kf_ref[rf, sl], v=vf_ref[rf, sl], al=alf_ref[rf, sl],
                               ab=abf_ref[rf, sl], lw=lwf_ref[rf, sl], c=cum_f[:, sl], ht_ref=hf_ref, idx=p,
                               y_ref=yf_ref, rows=rf, sl=sl))
            bodies.append(dict(dir='b', r=rb_ref[rb, sl], k=kb_ref[rb, sl], v=vb_ref[rb, sl], al=alb_ref[rb, sl],
                               ab=abb_ref[rb, sl], lw=lwb_ref[rb, sl], c=cum_b[:, sl], ht_ref=hb_ref, idx=p,
                               y_ref=yb_ref, rows=rb, sl=sl))
        _rw_prepare(bodies, masks, eye, lane_a)
        rounds.append(bodies)
    for bodies in rounds:
        _rw_advance(bodies)


def _rw_out_kernel(yf_ref, yb_ref, bonus_ref, g_ref, h_ref, gnw_ref, gnb_ref, e_ref, et_ref, w_ref, out_ref,
                   *, seqs, tm):
    y = yf_ref[...] + yb_ref[...]
    inv_n = 1.0 / RW_HEAD_DIM
    mean = _head_sum(y, e_ref, et_ref) * inv_n
    dlt = y - mean
    var = _head_sum(dlt * dlt, e_ref, et_ref) * inv_n
    yn = dlt * lax.rsqrt(var + RW_GN_EPS) * gnw_ref[...] + gnb_ref[...]
    out = (yn + bonus_ref[...]) * g_ref[...]
    res = h_ref[...] + _bdot(out, w_ref[...])
    valid = _valid_rows(pl.program_id(0) * tm, tm, seqs)
    out_ref[...] = jnp.where(valid, res, 0.0)


def _rwkv7_layer(h, norm_g, mu, w_r, w_k, w_v, w_o, w0, w1, w2, a0, a1, a2, g1, g2, k_k, k_a, r_k, gn_w, gn_b,
                 seqs, tm_in, tm_out, pairs):
    n_rows = h.shape[0]
    n_chunks = n_rows // CHUNK
    flat = jax.ShapeDtypeStruct((n_rows, D_MODEL), F32)
    vec = _const_spec((1, D_MODEL))
    vec2 = _const_spec((1, 2 * D_MODEL))
    row1 = lambda t: t.reshape(1, -1)

    def cat_dirs(w):
        return jnp.concatenate([w[0], w[1]], axis=1).astype(BF16)

    def blockdiag_dirs(w):
        z = jnp.zeros_like(w[0])
        return jnp.concatenate([jnp.concatenate([w[0], z], axis=1),
                                jnp.concatenate([z, w[1]], axis=1)], axis=0).astype(BF16)

    lora = g1.shape[1]
    g1p = jnp.pad(g1, ((0, 0), (0, RW_GATE_LORA_PAD - lora))).astype(BF16)
    g2p = jnp.pad(g2, ((0, RW_GATE_LORA_PAD - lora), (0, 0))).astype(BF16)
    w1c, w2c, a1c, a2c = cat_dirs(w1), blockdiag_dirs(w2), cat_dirs(a1), blockdiag_dirs(a2)
    e = (lax.broadcasted_iota(jnp.int32, (D_MODEL, LANES), 0) // RW_HEAD_DIM
         == lax.broadcasted_iota(jnp.int32, (D_MODEL, LANES), 1)).astype(BF16)
    et = e.T

    n_tiles = n_rows // tm_in
    prev, cur, nxt = _halo_specs(tm_in, n_rows)
    row = _row_spec(tm_in)
    sq = _const_spec((D_MODEL, D_MODEL))
    outs = pl.pallas_call(
        functools.partial(_rw_in_kernel, seqs=seqs, tm=tm_in, n_tiles=n_tiles),
        grid=(n_tiles,),
        in_specs=[prev, cur, nxt, vec, _const_spec((6, D_MODEL)), sq, sq, sq,
                  _const_spec(g1p.shape), _const_spec(g2p.shape),
                  vec2, _const_spec(w1c.shape), _const_spec(w2c.shape),
                  vec2, _const_spec(a1c.shape), _const_spec(a2c.shape),
                  vec, vec, vec, _const_spec(e.shape), _const_spec(et.shape)],
        out_specs=[row] * 11,
        out_shape=[flat] * 11,
        compiler_params=_params("parallel"),
        name="rw_in",
    )(h, h, h, row1(norm_g), mu, w_r.astype(BF16), w_k.astype(BF16), w_v.astype(BF16), g1p, g2p,
      row1(w0), w1c, w2c, row1(a0), a1c, a2c, row1(k_k), row1(k_a), row1(r_k), e, et)
    r, v, kkn, g, bonus, lwf, kdf, abf, lwb, kdb, abb = outs

    width = pairs * LANES
    n_groups = D_MODEL // width
    span = RW_CHUNKS_PER_STEP * CHUNK
    n_steps = n_rows // span
    fwd_starts = tuple(s // span for s, _, _ in seqs)
    bwd_starts = tuple(n_steps - (s + p) // span for s, _, p in seqs)
    cf = pl.BlockSpec((span, width), lambda p, c: (c, p))
    cb = pl.BlockSpec((span, width), lambda p, c: (n_steps - 1 - c, p))
    state = pltpu.VMEM((pairs, LANES, LANES), F32)
    y_f, y_b = pl.pallas_call(
        functools.partial(_rw_scan_kernel, fwd_starts=fwd_starts, bwd_starts=bwd_starts, pairs=pairs,
                          chunks=RW_CHUNKS_PER_STEP),
        grid=(n_groups, n_steps),
        in_specs=[cf] * 6 + [cb] * 6,
        out_specs=[cf, cb],
        out_shape=[flat, flat],
        scratch_shapes=[state, state],
        compiler_params=_params("parallel", "arbitrary"),
        name="rw_scan",
    )(r, kdf, v, kkn, abf, lwf, r, kdb, v, kkn, abb, lwb)

    row = _row_spec(tm_out)
    return pl.pallas_call(
        functools.partial(_rw_out_kernel, seqs=seqs, tm=tm_out),
        grid=(n_rows // tm_out,),
        in_specs=[row, row, row, row, row, vec, vec, _const_spec(e.shape), _const_spec(et.shape), sq],
        out_specs=row,
        out_shape=flat,
        compiler_params=_params("parallel"),
        name="rw_out",
    )(y_f, y_b, bonus, g, h, row1(gn_w), row1(gn_b), e, et, w_o.astype(BF16))


def _ffn_kernel(hp_ref, h_ref, hn_ref, g_ref, wu_ref, wv_ref, cw_ref, cb_ref, wo_ref, gf_ref, out_ref,
                *, seqs, tm, n_tiles, final):
    he = _with_halo(hp_ref, h_ref, hn_ref, n_tiles)
    xe = _rms(he, g_ref[...])
    ue = _bdot(xe, wu_ref[...])
    v = _bdot(xe[HALO:HALO + tm], wv_ref[...])
    um1, u0, up1 = _shift_rows(ue, tm)
    cw = cw_ref[...]
    u = um1 * cw[0:1] + u0 * cw[1:2] + up1 * cw[2:3] + cb_ref[...]
    y = he[HALO:HALO + tm] + _bdot(jax.nn.gelu(u) * v, wo_ref[...])
    valid = _valid_rows(pl.program_id(0) * tm, tm, seqs)
    y = jnp.where(valid, y, 0.0)
    if final:
        y = _rms(y, gf_ref[...])
    out_ref[...] = y


def _ffn_layer(h, norm_g, w_in, conv_w, conv_b, w_out, final_g, seqs, tm, final):
    n_rows = h.shape[0]
    n_tiles = n_rows // tm
    prev, cur, nxt = _halo_specs(tm, n_rows)
    vec = _const_spec((1, D_MODEL))
    vecf = _const_spec((1, D_FF))
    return pl.pallas_call(
        functools.partial(_ffn_kernel, seqs=seqs, tm=tm, n_tiles=n_tiles, final=final),
        grid=(n_tiles,),
        in_specs=[prev, cur, nxt, vec, _const_spec((D_MODEL, D_FF)), _const_spec((D_MODEL, D_FF)),
                  _const_spec((3, D_FF)), vecf, _const_spec((D_FF, D_MODEL)), vec],
        out_specs=_row_spec(tm),
        out_shape=jax.ShapeDtypeStruct((n_rows, D_MODEL), F32),
        compiler_params=_params("parallel"),
        name="ffn",
    )(h, h, h, norm_g.reshape(1, -1), w_in[:, :D_FF].astype(BF16), w_in[:, D_FF:].astype(BF16),
      conv_w, conv_b.reshape(1, -1), w_out.astype(BF16), final_g.reshape(1, -1))


def _layout(lengths):
    seqs, start = [], 0
    for n in lengths:
        padded = -(-n // ROW_ALIGN) * ROW_ALIGN
        seqs.append((start, n, padded))
        start += padded
    return tuple(seqs), start


def _trunk(xs, p, tm=256, tm_rw_in=256, rw_pairs=8):
    seqs, n_rows = _layout([N_META + x.shape[0] for x in xs])
    meta = p['meta_tokens'].astype(F32)
    pieces = []
    for x, (_, n, padded) in zip(xs, seqs):
        pieces += [meta, x, jnp.zeros((padded - n, D_MODEL), F32)]
    h = jnp.concatenate(pieces, axis=0)
    lb_all = jnp.cumsum(jax.nn.softmax(p['hg_lower_bound'].astype(F32), axis=0), axis=0)
    depth = p['norm_mix'].shape[0]
    for layer in range(depth):
        j = layer // N_MIXERS
        if layer % N_MIXERS == 0:
            h = _hgrn2_layer(h, p['norm_mix'][layer], p['hg_w_in'][j], p['hg_w_out'][j], p['hg_onorm'][j],
                             lb_all[j], seqs, tm)
        else:
            h = _rwkv7_layer(h, p['norm_mix'][layer], p['rw_mu'][j], p['rw_w_r'][j], p['rw_w_k'][j],
                             p['rw_w_v'][j], p['rw_w_o'][j], p['rw_w0'][j], p['rw_w1'][j], p['rw_w2'][j],
                             p['rw_a0'][j], p['rw_a1'][j], p['rw_a2'][j], p['rw_g1'][j], p['rw_g2'][j],
                             p['rw_k_k'][j], p['rw_k_a'][j], p['rw_r_k'][j], p['rw_gn_w'][j], p['rw_gn_b'][j],
                             seqs, tm_rw_in, tm, rw_pairs)
        h = _ffn_layer(h, p['norm_ffn'][layer], p['ffn_w_in'][layer], p['ffn_conv_w'][layer],
                       p['ffn_conv_b'][layer], p['ffn_w_out'][layer], p['norm_final'], seqs, tm,
                       final=(layer == depth - 1))
    return [h[s + N_META:s + n] for s, n, _ in seqs]


def kernel(x_prompt, x_sample, meta_tokens, norm_mix, norm_ffn, norm_final, hg_w_in, hg_w_out, hg_lower_bound,
           hg_onorm, rw_mu, rw_w_r, rw_w_k, rw_w_v, rw_w_o, rw_w0, rw_w1, rw_w2, rw_a0, rw_a1, rw_a2, rw_g1,
           rw_g2, rw_k_k, rw_k_a, rw_r_k, rw_gn_w, rw_gn_b, ffn_w_in, ffn_conv_w, ffn_conv_b, ffn_w_out):
    p = {
        'meta_tokens': meta_tokens, 'norm_mix': norm_mix, 'norm_ffn': norm_ffn, 'norm_final': norm_final,
        'hg_w_in': hg_w_in, 'hg_w_out': hg_w_out, 'hg_lower_bound': hg_lower_bound, 'hg_onorm': hg_onorm,
        'rw_mu': rw_mu, 'rw_w_r': rw_w_r, 'rw_w_k': rw_w_k, 'rw_w_v': rw_w_v, 'rw_w_o': rw_w_o,
        'rw_w0': rw_w0, 'rw_w1': rw_w1, 'rw_w2': rw_w2, 'rw_a0': rw_a0, 'rw_a1': rw_a1, 'rw_a2': rw_a2,
        'rw_g1': rw_g1, 'rw_g2': rw_g2, 'rw_k_k': rw_k_k, 'rw_k_a': rw_k_a, 'rw_r_k': rw_r_k,
        'rw_gn_w': rw_gn_w, 'rw_gn_b': rw_gn_b,
        'ffn_w_in': ffn_w_in, 'ffn_conv_w': ffn_conv_w, 'ffn_conv_b': ffn_conv_b, 'ffn_w_out': ffn_w_out,
    }
    xs = [x_prompt[b] for b in range(x_prompt.shape[0])] + [x_sample[b] for b in range(x_sample.shape[0])]
    ys = _trunk(xs, p)
    n_p = x_prompt.shape[0]
    return (jnp.stack(ys[:n_p], axis=0), jnp.stack(ys[n_p:], axis=0))
```

```python
import functools

import jax
import jax.numpy as jnp
from jax import lax
from jax.experimental import pallas as pl
from jax.experimental.pallas import tpu as pltpu

F32 = jnp.float32
BF16 = jnp.bfloat16
HIGHEST = lax.Precision.HIGHEST

D_MODEL = 1024
D_FF = 2816
N_META = 16
N_MIXERS = 2
HG_HEADS = 8
HG_HEAD_DIM = 128
RW_HEADS = 16
RW_HEAD_DIM = 64
RW_GATE_LORA_PAD = 256
NORM_EPS = 1e-6
RW_GN_EPS = 64e-5
RW_DECAY_SCALE = 0.6065306597126334

LANES = 128
HALO = 8
CHUNK = 64
HG_CHUNKS_PER_STEP = 4
RW_CHUNKS_PER_STEP = 2
ROW_ALIGN = 256
VMEM_LIMIT_BYTES = 56 * 1024 * 1024

NT_DIMS = (((1,), (1,)), ((), ()))


def _const_spec(shape):
    nd = len(shape)
    return pl.BlockSpec(shape, lambda *_: (0,) * nd, pipeline_mode=pl.Buffered(1))


def _params(*sem):
    return pltpu.CompilerParams(dimension_semantics=sem, vmem_limit_bytes=VMEM_LIMIT_BYTES)


def _valid_rows(row0, n, seqs):
    r = row0 + lax.broadcasted_iota(jnp.int32, (n, 1), 0)
    m = None
    for start, length, _ in seqs:
        t = (r >= start) & (r < start + length)
        m = t if m is None else (m | t)
    return m


def _rms(x, g):
    return x * lax.rsqrt(jnp.mean(x * x, axis=-1, keepdims=True) + NORM_EPS) * g


def _bdot(a, b):
    return jnp.dot(a.astype(BF16), b.astype(BF16), preferred_element_type=F32)


def _bdot_nt(a, b):
    return lax.dot_general(a.astype(BF16), b.astype(BF16), NT_DIMS, preferred_element_type=F32)


RW_INV_TERMS = 1
RW_INV_BASE = 4


def _split(x, terms):
    hi = x.astype(BF16)
    if terms == 1:
        return (hi,)
    return hi, (x - hi.astype(F32)).astype(BF16)


def _mms(a, b, dims=None):
    if dims is None:
        d = lambda x, y: jnp.dot(x, y, preferred_element_type=F32)
    else:
        d = lambda x, y: lax.dot_general(x, y, dims, preferred_element_type=F32)
    out = d(a[0], b[0])
    if len(a) == 2 and len(b) == 2:
        out = out + (d(a[0], b[1]) + d(a[1], b[0]))
    return out


def _split3(x):
    x1 = x.astype(BF16)
    r1 = x - x1.astype(F32)
    x2 = r1.astype(BF16)
    x3 = (r1 - x2.astype(F32)).astype(BF16)
    return x1, x2, x3


def _dot_exact_rhs(a_bf16, x):
    x1, x2, x3 = _split3(x)
    d = lambda t: jnp.dot(a_bf16, t, preferred_element_type=F32)
    return d(x1) + d(x2) + d(x3)


def _head_sum(x, e_ref, et_ref):
    def dot2(z, ind):
        z1, z2 = _split(z, 2)
        return (jnp.dot(z1, ind, preferred_element_type=F32) + jnp.dot(z2, ind, preferred_element_type=F32))

    s = dot2(x, e_ref[...])
    return dot2(s, et_ref[...])


def _shift_rows(xe, tm):
    n = xe.shape[0]
    prev = pltpu.roll(xe, 1, axis=0)[HALO:HALO + tm]
    nxt = pltpu.roll(xe, n - 1, axis=0)[HALO:HALO + tm]
    return prev, xe[HALO:HALO + tm], nxt


def _with_halo(hp_ref, h_ref, hn_ref, n_tiles):
    i = pl.program_id(0)
    hp = jnp.where(i > 0, hp_ref[...], 0.0)
    hn = jnp.where(i < n_tiles - 1, hn_ref[...], 0.0)
    return jnp.concatenate([hp, h_ref[...], hn], axis=0)


def _halo_specs(tm, n_rows):
    r = tm // HALO
    last = n_rows // HALO - 1
    prev = pl.BlockSpec((HALO, D_MODEL), lambda i: (jnp.maximum(i * r - 1, 0), 0))
    cur = pl.BlockSpec((tm, D_MODEL), lambda i: (i, 0))
    nxt = pl.BlockSpec((HALO, D_MODEL), lambda i: (jnp.minimum((i + 1) * r, last), 0))
    return prev, cur, nxt


def _row_spec(tm, width=D_MODEL):
    return pl.BlockSpec((tm, width), lambda i: (i, 0))


def _chunk_masks():
    t = lax.broadcasted_iota(jnp.int32, (CHUNK, CHUNK), 0)
    s = lax.broadcasted_iota(jnp.int32, (CHUNK, CHUNK), 1)
    return t, s


def _hg_in_kernel(h_ref, g_ref, w_ref, lb_ref, q_ref, kf_ref, lff_ref, kb_ref, lfb_ref, i_ref, gs_ref,
                  *, seqs, tm):
    hn = _rms(h_ref[...], g_ref[...]).astype(BF16)
    proj = lambda j: jnp.dot(hn, w_ref[:, j * D_MODEL:(j + 1) * D_MODEL], preferred_element_type=F32)
    valid = _valid_rows(pl.program_id(0) * tm, tm, seqs).astype(F32)
    lb = lb_ref[...]
    q = proj(0)
    q_ref[...] = q * jax.nn.sigmoid(q)
    for j, k_ref, lf_ref in ((1, kf_ref, lff_ref), (2, kb_ref, lfb_ref)):
        sg = jax.nn.sigmoid(proj(j))
        lf_ref[...] = jnp.log(lb + (1.0 - lb) * sg) * valid
        k_ref[...] = (1.0 - lb) * (1.0 - sg) * valid
    i_ref[...] = proj(3)
    g = proj(4)
    gs_ref[...] = g * jax.nn.sigmoid(g)


def _hg_chunks(rounds, masks):
    for bodies in rounds:
        for b in bodies:
            m = masks[b['dir']]
            cum = b['cum']
            c_ref = cum[m['mid']:m['mid'] + 1]
            c_last = cum[m['last']:m['last'] + 1]
            q, k = b['q'], b['k']
            b.update(q_in=q * jnp.exp(cum - c_ref), k_in=k * jnp.exp(c_ref - cum), q_st=q * jnp.exp(cum),
                     k_st=k * jnp.exp(c_last - cum), g=jnp.exp(c_last))
        for b in bodies:
            b['scores'] = jnp.where(masks[b['dir']]['incl'], _bdot_nt(b['q_in'], b['k_in']), 0.0)
    for bodies in rounds:
        for b in bodies:
            b['st'] = b['st_ref'][b['idx']]
            b['o_ref'][b['rows'], b['sl']] = _bdot(b['scores'], b['iv']) + _bdot_nt(b['q_st'], b['st'])
        for b in bodies:
            b['st_ref'][b['idx']] = b['st'] * b['g'] + _bdot(b['iv'].T, b['k_st'])


def _hg_scan_kernel(q_ref, k_ref, lf_ref, i_ref, *rest, direction, starts, chunks, seqs):
    if direction == 'f':
        ob_ref, gs_ref, h_ref, on_ref, w_ref, out_ref, st_ref, o_ref = rest
    else:
        o_ref, st_ref = rest
    c = pl.program_id(0)

    @pl.when(functools.reduce(jnp.logical_or, [c == s for s in starts]))
    def _():
        st_ref[...] = jnp.zeros(st_ref.shape, F32)

    t, s = _chunk_masks()
    if direction == 'f':
        masks = {'f': dict(incl=s <= t, mid=CHUNK // 2, last=CHUNK - 1)}
        tri = (s <= t).astype(BF16)
    else:
        masks = {'b': dict(incl=s >= t, mid=CHUNK // 2 - 1, last=0)}
        tri = (s >= t).astype(BF16)
    rounds = []
    for j in range(chunks):
        jj = j if direction == 'f' else chunks - 1 - j
        rows = slice(jj * CHUNK, (jj + 1) * CHUNK)
        cum = _dot_exact_rhs(tri, lf_ref[rows, :])
        rounds.append([dict(dir=direction, q=q_ref[rows, sl], k=k_ref[rows, sl], iv=i_ref[rows, sl], cum=cum[:, sl],
                            st_ref=st_ref, idx=hd, o_ref=o_ref, rows=rows, sl=sl)
                       for hd, sl in enumerate(slice(i * HG_HEAD_DIM, (i + 1) * HG_HEAD_DIM)
                                               for i in range(HG_HEADS))])
    _hg_chunks(rounds, masks)
    if direction == 'f':
        span = chunks * CHUNK
        o = o_ref[...] + ob_ref[...]
        parts = []
        for hd in range(HG_HEADS):
            oh = o[:, hd * HG_HEAD_DIM:(hd + 1) * HG_HEAD_DIM]
            parts.append(oh * lax.rsqrt(jnp.mean(oh * oh, axis=-1, keepdims=True) + NORM_EPS))
        on = jnp.concatenate(parts, axis=1) * on_ref[...] * gs_ref[...]
        y = h_ref[...] + _bdot(on, w_ref[...])
        out_ref[...] = jnp.where(_valid_rows(c * span, span, seqs), y, 0.0)


def _hgrn2_layer(h, norm_g, w_in, w_out, onorm_g, lb, seqs, tm):
    n_rows = h.shape[0]
    n_tiles = n_rows // tm
    n_chunks = n_rows // CHUNK
    row = _row_spec(tm)
    vec = _const_spec((1, D_MODEL))
    flat = jax.ShapeDtypeStruct((n_rows, D_MODEL), F32)
    q, kf, lff, kb, lfb, iv, gs = pl.pallas_call(
        functools.partial(_hg_in_kernel, seqs=seqs, tm=tm),
        grid=(n_tiles,),
        in_specs=[row, vec, _const_spec((D_MODEL, 5 * D_MODEL)), vec],
        out_specs=[row] * 7,
        out_shape=[flat] * 7,
        compiler_params=_params("parallel"),
        name="hg_in",
    )(h, norm_g.reshape(1, -1), w_in.astype(BF16), lb.reshape(1, -1))

    span = HG_CHUNKS_PER_STEP * CHUNK
    n_steps = n_rows // span
    fwd_starts = tuple(s // span for s, _, _ in seqs)
    bwd_starts = tuple(n_steps - (s + p) // span for s, _, p in seqs)
    cf = pl.BlockSpec((span, D_MODEL), lambda c: (c, 0))
    cb = pl.BlockSpec((span, D_MODEL), lambda c: (n_steps - 1 - c, 0))
    state = pltpu.VMEM((HG_HEADS, HG_HEAD_DIM, HG_HEAD_DIM), F32)
    scan = functools.partial(_hg_scan_kernel, chunks=HG_CHUNKS_PER_STEP, seqs=seqs)
    o_b = pl.pallas_call(
        functools.partial(scan, direction='b', starts=bwd_starts),
        grid=(n_steps,),
        in_specs=[cb] * 4,
        out_specs=cb,
        out_shape=flat,
        scratch_shapes=[state],
        compiler_params=_params("arbitrary"),
        name="hg_scan_b",
    )(q, kb, lfb, iv)
    return pl.pallas_call(
        functools.partial(scan, direction='f', starts=fwd_starts),
        grid=(n_steps,),
        in_specs=[cf] * 7 + [vec, _const_spec((D_MODEL, D_MODEL))],
        out_specs=cf,
        out_shape=flat,
        scratch_shapes=[state, pltpu.VMEM((span, D_MODEL), F32)],
        compiler_params=_params("arbitrary"),
        name="hg_scan_f",
    )(q, kf, lff, iv, o_b, gs, h, onorm_g.reshape(1, -1), w_out.astype(BF16))


def _rw_in_kernel(hp_ref, h_ref, hn_ref, g_ref, mu_ref, wr_ref, wk_ref, wv_ref, g1_ref, g2_ref,
                  w0_ref, w1_ref, w2_ref, a0_ref, a1_ref, a2_ref, kk_ref, ka_ref, rk_ref, e_ref, et_ref,
                  r_out, v_out, kkn_out, g_out, bonus_out,
                  lwf_out, kdf_out, abf_out, lwb_out, kdb_out, abb_out, *, seqs, tm, n_tiles):
    xe = _rms(_with_halo(hp_ref, h_ref, hn_ref, n_tiles), g_ref[...])
    xm1, x, xp1 = _shift_rows(xe, tm)
    xx = 0.5 * (xm1 + xp1) - x
    mu = mu_ref[...]
    xr, xw, xk, xv, xa, xg = (x + xx * mu[j:j + 1] for j in range(6))
    r = _bdot(xr, wr_ref[...])
    k = _bdot(xk, wk_ref[...])
    v = _bdot(xv, wv_ref[...])
    g_out[...] = _bdot(jax.nn.sigmoid(_bdot(xg, g1_ref[...])), g2_ref[...])
    z = w0_ref[...] + _bdot(jnp.tanh(_bdot(xw, w1_ref[...])), w2_ref[...])
    lw = -RW_DECAY_SCALE * jax.nn.sigmoid(z)
    a = jax.nn.sigmoid(a0_ref[...] + _bdot(_bdot(xa, a1_ref[...]), a2_ref[...]))
    kkr = k * kk_ref[...]
    ssq = _head_sum(kkr * kkr, e_ref, et_ref)
    kkn = kkr * lax.rsqrt(jnp.maximum(ssq, 1e-24))
    valid = _valid_rows(pl.program_id(0) * tm, tm, seqs).astype(F32)
    ka = ka_ref[...]
    kd_sum = None
    for d, (lw_o, kd_o, ab_o) in enumerate(((lwf_out, kdf_out, abf_out), (lwb_out, kdb_out, abb_out))):
        a_d = a[:, d * D_MODEL:(d + 1) * D_MODEL]
        kd = k * (1.0 + (a_d - 1.0) * ka)
        kd_sum = kd if kd_sum is None else kd_sum + kd
        lw_o[...] = lw[:, d * D_MODEL:(d + 1) * D_MODEL]
        kd_o[...] = kd * valid
        ab_o[...] = kkn * a_d
    r_out[...] = r
    v_out[...] = v
    kkn_out[...] = kkn
    bonus_out[...] = _head_sum(r * kd_sum * rk_ref[...], e_ref, et_ref) * v


def _rw_prepare(bodies, masks, eye, lane_a):
    stack = lambda z_: jnp.concatenate([jnp.where(lane_a, z_, 0.0), jnp.where(lane_a, 0.0, z_)], axis=0)
    sp1 = lambda z_: _split(z_, 1)
    spi = lambda z_: _split(z_, RW_INV_TERMS)
    n = 2 * CHUNK
    for b in bodies:
        m = masks[b['dir']]
        c, lw = b['c'], b['lw']
        e = c - lw
        c_mid = c[m['mid']:m['mid'] + 1]
        c_last = c[m['last']:m['last'] + 1]
        beta = -b['ab']
        dec_out = jnp.exp(c_mid - c)
        dec_end = jnp.exp(c_last - c)
        vs = stack(b['v'])
        b.update(
            lhs_a=spi(stack(b['al'] * jnp.exp(e - c_mid))), lhs_r=sp1(stack(b['r'] * jnp.exp(c - c_mid))),
            rhs=spi(jnp.concatenate([beta * dec_out] * 2 + [b['k'] * dec_out] * 2, axis=0)),
            vs=sp1(vs), vs_t=sp1(vs.T), al_h=sp1(stack(b['al'] * jnp.exp(e))), r_h=sp1(stack(b['r'] * jnp.exp(c))),
            b_h=sp1(stack(beta * dec_end)), k_h=sp1(stack(b['k'] * dec_end)), g=jnp.exp(c_last))
    for b in bodies:
        b['sc_a'] = _mms(b['lhs_a'], b['rhs'], NT_DIMS)
    for b in bodies:
        b['sc_r'] = _mms(b['lhs_r'], b['rhs'][:1], NT_DIMS)
    for b in bodies:
        m = masks[b['dir']]
        sc_a, sc_r = b['sc_a'], b['sc_r']
        b.update(
            a_d=jnp.where(m['diag'], sc_a[:, :n], 0.0),
            a_off=[spi(jnp.where(mo, sc_a[:, :n], 0.0)) for mo in m['off']],
            a_ak=sp1(jnp.where(m['strict'], sc_a[:, n:], 0.0)),
            a_rb=sp1(jnp.where(m['incl'], sc_r[:, :n], 0.0)), a_rk=sp1(jnp.where(m['incl'], sc_r[:, n:], 0.0)))
    for b in bodies:
        b['pw'] = spi(b['a_d'])
        b['inv'] = eye + b['a_d']
    for _ in range(RW_INV_BASE.bit_length() - 2):
        for b in bodies:
            b['pw'] = spi(_mms(b['pw'], b['pw']))
        for b in bodies:
            b['inv'] = b['inv'] + _mms(spi(b['inv']), b['pw'])
    for level in range(len(masks['f']['off'])):
        for b in bodies:
            b['inv_s'] = spi(b['inv'])
            b['t'] = _mms(b['a_off'][level], b['inv_s'])
        for b in bodies:
            b['inv'] = b['inv'] + _mms(b['inv_s'], spi(b['t']))
    for b in bodies:
        b['inv16'] = b['inv'].astype(BF16)


def _rw_advance(bodies):
    for b in bodies:
        b['ht'] = b['ht_ref'][b['idx']]
        b['h16'] = b['ht'].T.astype(BF16)
        b['x'] = jnp.dot(jnp.concatenate([b['al_h'][0], b['a_ak'][0]], axis=1),
                         jnp.concatenate([b['h16'], b['vs'][0]], axis=0), preferred_element_type=F32)
    for b in bodies:
        b['u'] = jnp.dot(b['inv16'], b['x'].astype(BF16), preferred_element_type=F32)
    for b in bodies:
        y = jnp.dot(jnp.concatenate([b['r_h'][0], b['a_rb'][0], b['a_rk'][0]], axis=1),
                    jnp.concatenate([b['h16'], b['u'].astype(BF16), b['vs'][0]], axis=0),
                    preferred_element_type=F32)
        b['y_ref'][b['rows'], b['sl']] = y[:CHUNK] + y[CHUNK:]
    for b in bodies:
        b['ht_ref'][b['idx']] = b['ht'] * b['g'] + jnp.dot(
            jnp.concatenate([b['u'].T.astype(BF16), b['vs_t'][0]], axis=1),
            jnp.concatenate([b['b_h'][0], b['k_h'][0]], axis=0), preferred_element_type=F32)


def _rw_scan_kernel(rf_ref, kf_ref, vf_ref, alf_ref, abf_ref, lwf_ref,
                    rb_ref, kb_ref, vb_ref, alb_ref, abb_ref, lwb_ref,
                    yf_ref, yb_ref, hf_ref, hb_ref, *, fwd_starts, bwd_starts, pairs, chunks):
    c = pl.program_id(1)

    def reset_at(ref, steps):
        hit = functools.reduce(jnp.logical_or, [c == s for s in steps])

        @pl.when(hit)
        def _():
            ref[...] = jnp.zeros(ref.shape, F32)

    reset_at(hf_ref, fwd_starts)
    reset_at(hb_ref, bwd_starts)
    t, s = _chunk_masks()
    n = 2 * CHUNK
    ti = lax.broadcasted_iota(jnp.int32, (n, n), 0)
    si = lax.broadcasted_iota(jnp.int32, (n, n), 1)
    same = lambda width: (ti // width) == (si // width)
    eye = (ti == si).astype(F32)
    lane_a = lax.broadcasted_iota(jnp.int32, (CHUNK, LANES), 1) < RW_HEAD_DIM

    widths = [RW_INV_BASE << j for j in range((CHUNK // RW_INV_BASE).bit_length() - 1)]

    def dir_masks(before, before_incl, mid, last):
        return dict(mid=mid, last=last,
                    strict=same(CHUNK) & before, incl=same(CHUNK) & before_incl,
                    diag=same(RW_INV_BASE) & before,
                    off=[same(2 * w) & ~same(w) & before for w in widths])

    masks = {'f': dir_masks(si < ti, si <= ti, CHUNK // 2, CHUNK - 1),
             'b': dir_masks(si > ti, si >= ti, CHUNK // 2 - 1, 0)}
    tri_f, tri_b = (s <= t).astype(BF16), (s >= t).astype(BF16)
    rounds = []
    for j in range(chunks):
        rf = slice(j * CHUNK, (j + 1) * CHUNK)
        rb = slice((chunks - 1 - j) * CHUNK, (chunks - j) * CHUNK)
        cum_f = _dot_exact_rhs(tri_f, lwf_ref[rf, :])
        cum_b = _dot_exact_rhs(tri_b, lwb_ref[rb, :])
        bodies = []
        for p in range(pairs):
            sl = slice(p * LANES, (p + 1) * LANES)
            bodies.append(dict(dir='f', r=rf_ref[rf, sl], k=kf_ref[rf, sl], v=vf_ref[rf, sl], al=alf_ref[rf, sl],
                               ab=abf_ref[rf, sl], lw=lwf_ref[rf, sl], c=cum_f[:, sl], ht_ref=hf_ref, idx=p,
                               y_ref=yf_ref, rows=rf, sl=sl))
            bodies.append(dict(dir='b', r=rb_ref[rb, sl], k=kb_ref[rb, sl], v=vb_ref[rb, sl], al=alb_ref[rb, sl],
                               ab=abb_ref[rb, sl], lw=lwb_ref[rb, sl], c=cum_b[:, sl], ht_ref=hb_ref, idx=p,
                               y_ref=yb_ref, rows=rb, sl=sl))
        _rw_prepare(bodies, masks, eye, lane_a)
        rounds.append(bodies)
    for bodies in rounds:
        _rw_advance(bodies)


def _rw_out_kernel(yf_ref, yb_ref, bonus_ref, g_ref, h_ref, gnw_ref, gnb_ref, e_ref, et_ref, w_ref, out_ref,
                   *, seqs, tm):
    y = yf_ref[...] + yb_ref[...]
    inv_n = 1.0 / RW_HEAD_DIM
    mean = _head_sum(y, e_ref, et_ref) * inv_n
    dlt = y - mean
    var = _head_sum(dlt * dlt, e_ref, et_ref) * inv_n
    yn = dlt * lax.rsqrt(var + RW_GN_EPS) * gnw_ref[...] + gnb_ref[...]
    out = (yn + bonus_ref[...]) * g_ref[...]
    res = h_ref[...] + _bdot(out, w_ref[...])
    valid = _valid_rows(pl.program_id(0) * tm, tm, seqs)
    out_ref[...] = jnp.where(valid, res, 0.0)


def _rwkv7_layer(h, norm_g, mu, w_r, w_k, w_v, w_o, w0, w1, w2, a0, a1, a2, g1, g2, k_k, k_a, r_k, gn_w, gn_b,
                 seqs, tm_in, tm_out, pairs):
    n_rows = h.shape[0]
    n_chunks = n_rows // CHUNK
    flat = jax.ShapeDtypeStruct((n_rows, D_MODEL), F32)
    vec = _const_spec((1, D_MODEL))
    vec2 = _const_spec((1, 2 * D_MODEL))
    row1 = lambda t: t.reshape(1, -1)

    def cat_dirs(w):
        return jnp.concatenate([w[0], w[1]], axis=1).astype(BF16)

    def blockdiag_dirs(w):
        z = jnp.zeros_like(w[0])
        return jnp.concatenate([jnp.concatenate([w[0], z], axis=1),
                                jnp.concatenate([z, w[1]], axis=1)], axis=0).astype(BF16)

    lora = g1.shape[1]
    g1p = jnp.pad(g1, ((0, 0), (0, RW_GATE_LORA_PAD - lora))).astype(BF16)
    g2p = jnp.pad(g2, ((0, RW_GATE_LORA_PAD - lora), (0, 0))).astype(BF16)
    w1c, w2c, a1c, a2c = cat_dirs(w1), blockdiag_dirs(w2), cat_dirs(a1), blockdiag_dirs(a2)
    e = (lax.broadcasted_iota(jnp.int32, (D_MODEL, LANES), 0) // RW_HEAD_DIM
         == lax.broadcasted_iota(jnp.int32, (D_MODEL, LANES), 1)).astype(BF16)
    et = e.T

    n_tiles = n_rows // tm_in
    prev, cur, nxt = _halo_specs(tm_in, n_rows)
    row = _row_spec(tm_in)
    sq = _const_spec((D_MODEL, D_MODEL))
    outs = pl.pallas_call(
        functools.partial(_rw_in_kernel, seqs=seqs, tm=tm_in, n_tiles=n_tiles),
        grid=(n_tiles,),
        in_specs=[prev, cur, nxt, vec, _const_spec((6, D_MODEL)), sq, sq, sq,
                  _const_spec(g1p.shape), _const_spec(g2p.shape),
                  vec2, _const_spec(w1c.shape), _const_spec(w2c.shape),
                  vec2, _const_spec(a1c.shape), _const_spec(a2c.shape),
                  vec, vec, vec, _const_spec(e.shape), _const_spec(et.shape)],
        out_specs=[row] * 11,
        out_shape=[flat] * 11,
        compiler_params=_params("parallel"),
        name="rw_in",
    )(h, h, h, row1(norm_g), mu, w_r.astype(BF16), w_k.astype(BF16), w_v.astype(BF16), g1p, g2p,
      row1(w0), w1c, w2c, row1(a0), a1c, a2c, row1(k_k), row1(k_a), row1(r_k), e, et)
    r, v, kkn, g, bonus, lwf, kdf, abf, lwb, kdb, abb = outs

    width = pairs * LANES
    n_groups = D_MODEL // width
    span = RW_CHUNKS_PER_STEP * CHUNK
    n_steps = n_rows // span
    fwd_starts = tuple(s // span for s, _, _ in seqs)
    bwd_starts = tuple(n_steps - (s + p) // span for s, _, p in seqs)
    cf = pl.BlockSpec((span, width), lambda p, c: (c, p))
    cb = pl.BlockSpec((span, width), lambda p, c: (n_steps - 1 - c, p))
    state = pltpu.VMEM((pairs, LANES, LANES), F32)
    y_f, y_b = pl.pallas_call(
        functools.partial(_rw_scan_kernel, fwd_starts=fwd_starts, bwd_starts=bwd_starts, pairs=pairs,
                          chunks=RW_CHUNKS_PER_STEP),
        grid=(n_groups, n_steps),
        in_specs=[cf] * 6 + [cb] * 6,
        out_specs=[cf, cb],
        out_shape=[flat, flat],
        scratch_shapes=[state, state],
        compiler_params=_params("parallel", "arbitrary"),
        name="rw_scan",
    )(r, kdf, v, kkn, abf, lwf, r, kdb, v, kkn, abb, lwb)

    row = _row_spec(tm_out)
    return pl.pallas_call(
        functools.partial(_rw_out_kernel, seqs=seqs, tm=tm_out),
        grid=(n_rows // tm_out,),
        in_specs=[row, row, row, row, row, vec, vec, _const_spec(e.shape), _const_spec(et.shape), sq],
        out_specs=row,
        out_shape=flat,
        compiler_params=_params("parallel"),
        name="rw_out",
    )(y_f, y_b, bonus, g, h, row1(gn_w), row1(gn_b), e, et, w_o.astype(BF16))


def _ffn_kernel(hp_ref, h_ref, hn_ref, g_ref, wu_ref, wv_ref, cw_ref, cb_ref, wo_ref, gf_ref, out_ref,
                *, seqs, tm, n_tiles, final):
    he = _with_halo(hp_ref, h_ref, hn_ref, n_tiles)
    xe = _rms(he, g_ref[...])
    ue = _bdot(xe, wu_ref[...])
    v = _bdot(xe[HALO:HALO + tm], wv_ref[...])
    um1, u0, up1 = _shift_rows(ue, tm)
    cw = cw_ref[...]
    u = um1 * cw[0:1] + u0 * cw[1:2] + up1 * cw[2:3] + cb_ref[...]
    y = he[HALO:HALO + tm] + _bdot(jax.nn.gelu(u) * v, wo_ref[...])
    valid = _valid_rows(pl.program_id(0) * tm, tm, seqs)
    y = jnp.where(valid, y, 0.0)
    if final:
        y = _rms(y, gf_ref[...])
    out_ref[...] = y


def _ffn_layer(h, norm_g, w_in, conv_w, conv_b, w_out, final_g, seqs, tm, final):
    n_rows = h.shape[0]
    n_tiles = n_rows // tm
    prev, cur, nxt = _halo_specs(tm, n_rows)
    vec = _const_spec((1, D_MODEL))
    vecf = _const_spec((1, D_FF))
    return pl.pallas_call(
        functools.partial(_ffn_kernel, seqs=seqs, tm=tm, n_tiles=n_tiles, final=final),
        grid=(n_tiles,),
        in_specs=[prev, cur, nxt, vec, _const_spec((D_MODEL, D_FF)), _const_spec((D_MODEL, D_FF)),
                  _const_spec((3, D_FF)), vecf, _const_spec((D_FF, D_MODEL)), vec],
        out_specs=_row_spec(tm),
        out_shape=jax.ShapeDtypeStruct((n_rows, D_MODEL), F32),
        compiler_params=_params("parallel"),
        name="ffn",
    )(h, h, h, norm_g.reshape(1, -1), w_in[:, :D_FF].astype(BF16), w_in[:, D_FF:].astype(BF16),
      conv_w, conv_b.reshape(1, -1), w_out.astype(BF16), final_g.reshape(1, -1))


def _layout(lengths):
    seqs, start = [], 0
    for n in lengths:
        padded = -(-n // ROW_ALIGN) * ROW_ALIGN
        seqs.append((start, n, padded))
        start += padded
    return tuple(seqs), start


def _trunk(xs, p, tm=256, tm_rw_in=256, rw_pairs=8):
    seqs, n_rows = _layout([N_META + x.shape[0] for x in xs])
    meta = p['meta_tokens'].astype(F32)
    pieces = []
    for x, (_, n, padded) in zip(xs, seqs):
        pieces += [meta, x, jnp.zeros((padded - n, D_MODEL), F32)]
    h = jnp.concatenate(pieces, axis=0)
    lb_all = jnp.cumsum(jax.nn.softmax(p['hg_lower_bound'].astype(F32), axis=0), axis=0)
    depth = p['norm_mix'].shape[0]
    for layer in range(depth):
        j = layer // N_MIXERS
        if layer % N_MIXERS == 0:
            h = _hgrn2_layer(h, p['norm_mix'][layer], p['hg_w_in'][j], p['hg_w_out'][j], p['hg_onorm'][j],
                             lb_all[j], seqs, tm)
        else:
            h = _rwkv7_layer(h, p['norm_mix'][layer], p['rw_mu'][j], p['rw_w_r'][j], p['rw_w_k'][j],
                             p['rw_w_v'][j], p['rw_w_o'][j], p['rw_w0'][j], p['rw_w1'][j], p['rw_w2'][j],
                             p['rw_a0'][j], p['rw_a1'][j], p['rw_a2'][j], p['rw_g1'][j], p['rw_g2'][j],
                             p['rw_k_k'][j], p['rw_k_a'][j], p['rw_r_k'][j], p['rw_gn_w'][j], p['rw_gn_b'][j],
                             seqs, tm_rw_in, tm, rw_pairs)
        h = _ffn_layer(h, p['norm_ffn'][layer], p['ffn_w_in'][layer], p['ffn_conv_w'][layer],
                       p['ffn_conv_b'][layer], p['ffn_w_out'][layer], p['norm_final'], seqs, tm,
                       final=(layer == depth - 1))
    return [h[s + N_META:s + n] for s, n, _ in seqs]


def kernel(x_prompt, x_sample, meta_tokens, norm_mix, norm_ffn, norm_final, hg_w_in, hg_w_out, hg_lower_bound,
           hg_onorm, rw_mu, rw_w_r, rw_w_k, rw_w_v, rw_w_o, rw_w0, rw_w1, rw_w2, rw_a0, rw_a1, rw_a2, rw_g1,
           rw_g2, rw_k_k, rw_k_a, rw_r_k, rw_gn_w, rw_gn_b, ffn_w_in, ffn_conv_w, ffn_conv_b, ffn_w_out):
    p = {
        'meta_tokens': meta_tokens, 'norm_mix': norm_mix, 'norm_ffn': norm_ffn, 'norm_final': norm_final,
        'hg_w_in': hg_w_in, 'hg_w_out': hg_w_out, 'hg_lower_bound': hg_lower_bound, 'hg_onorm': hg_onorm,
        'rw_mu': rw_mu, 'rw_w_r': rw_w_r, 'rw_w_k': rw_w_k, 'rw_w_v': rw_w_v, 'rw_w_o': rw_w_o,
        'rw_w0': rw_w0, 'rw_w1': rw_w1, 'rw_w2': rw_w2, 'rw_a0': rw_a0, 'rw_a1': rw_a1, 'rw_a2': rw_a2,
        'rw_g1': rw_g1, 'rw_g2': rw_g2, 'rw_k_k': rw_k_k, 'rw_k_a': rw_k_a, 'rw_r_k': rw_r_k,
        'rw_gn_w': rw_gn_w, 'rw_gn_b': rw_gn_b,
        'ffn_w_in': ffn_w_in, 'ffn_conv_w': ffn_conv_w, 'ffn_conv_b': ffn_conv_b, 'ffn_w_out': ffn_w_out,
    }
    xs = [x_prompt[b] for b in range(x_prompt.shape[0])] + [x_sample[b] for b in range(x_sample.shape[0])]
    ys = _trunk(xs, p)
    n_p = x_prompt.shape[0]
    return (jnp.stack(ys[:n_p], axis=0), jnp.stack(ys[n_p:], axis=0))
```

```python
import functools

import jax
import jax.numpy as jnp
from jax import lax
from jax.experimental import pallas as pl
from jax.experimental.pallas import tpu as pltpu

F32 = jnp.float32
BF16 = jnp.bfloat16
HIGHEST = lax.Precision.HIGHEST

D_MODEL = 1024
D_FF = 2816
N_META = 16
N_MIXERS = 2
HG_HEADS = 8
HG_HEAD_DIM = 128
RW_HEADS = 16
RW_HEAD_DIM = 64
RW_GATE_LORA_PAD = 256
NORM_EPS = 1e-6
RW_GN_EPS = 64e-5
RW_DECAY_SCALE = 0.6065306597126334

LANES = 128
HALO = 8
CHUNK = 64
HG_CHUNKS_PER_STEP = 4
RW_CHUNKS_PER_STEP = 4
ROW_ALIGN = 256
VMEM_LIMIT_BYTES = 56 * 1024 * 1024

NT_DIMS = (((1,), (1,)), ((), ()))


def _const_spec(shape):
    nd = len(shape)
    return pl.BlockSpec(shape, lambda *_: (0,) * nd, pipeline_mode=pl.Buffered(1))


def _params(*sem):
    return pltpu.CompilerParams(dimension_semantics=sem, vmem_limit_bytes=VMEM_LIMIT_BYTES)


def _valid_rows(row0, n, seqs):
    r = row0 + lax.broadcasted_iota(jnp.int32, (n, 1), 0)
    m = None
    for _, first, length, _ in seqs:
        t = (r >= first) & (r < first + length)
        m = t if m is None else (m | t)
    return m


def _rms(x, g):
    return x * lax.rsqrt(jnp.mean(x * x, axis=-1, keepdims=True) + NORM_EPS) * g


def _bdot(a, b):
    return jnp.dot(a.astype(BF16), b.astype(BF16), preferred_element_type=F32)


def _bdot_nt(a, b):
    return lax.dot_general(a.astype(BF16), b.astype(BF16), NT_DIMS, preferred_element_type=F32)


RW_INV_BASE = 4


def _split(x, terms):
    hi = x.astype(BF16)
    if terms == 1:
        return (hi,)
    return hi, (x - hi.astype(F32)).astype(BF16)


def _split3(x):
    x1 = x.astype(BF16)
    r1 = x - x1.astype(F32)
    x2 = r1.astype(BF16)
    x3 = (r1 - x2.astype(F32)).astype(BF16)
    return x1, x2, x3


def _dot_exact_rhs(a_bf16, x):
    x1, x2, x3 = _split3(x)
    d = lambda t: jnp.dot(a_bf16, t, preferred_element_type=F32)
    return d(x1) + d(x2) + d(x3)


def _head_sum(x, e_ref, et_ref):
    def dot2(z, ind):
        z1, z2 = _split(z, 2)
        return (jnp.dot(z1, ind, preferred_element_type=F32) + jnp.dot(z2, ind, preferred_element_type=F32))

    s = dot2(x, e_ref[...])
    return dot2(s, et_ref[...])


def _shift_rows(xe, tm):
    n = xe.shape[0]
    prev = pltpu.roll(xe, 1, axis=0)[HALO:HALO + tm]
    nxt = pltpu.roll(xe, n - 1, axis=0)[HALO:HALO + tm]
    return prev, xe[HALO:HALO + tm], nxt


def _with_halo(hp_ref, h_ref, hn_ref, n_tiles):
    i = pl.program_id(0)
    hp = jnp.where(i > 0, hp_ref[...], 0.0)
    hn = jnp.where(i < n_tiles - 1, hn_ref[...], 0.0)
    return jnp.concatenate([hp, h_ref[...], hn], axis=0)


def _halo_specs(tm, n_rows):
    r = tm // HALO
    last = n_rows // HALO - 1
    prev = pl.BlockSpec((HALO, D_MODEL), lambda i: (jnp.maximum(i * r - 1, 0), 0))
    cur = pl.BlockSpec((tm, D_MODEL), lambda i: (i, 0))
    nxt = pl.BlockSpec((HALO, D_MODEL), lambda i: (jnp.minimum((i + 1) * r, last), 0))
    return prev, cur, nxt


def _row_spec(tm, width=D_MODEL):
    return pl.BlockSpec((tm, width), lambda i: (i, 0))


def _chunk_masks():
    t = lax.broadcasted_iota(jnp.int32, (CHUNK, CHUNK), 0)
    s = lax.broadcasted_iota(jnp.int32, (CHUNK, CHUNK), 1)
    return t, s


def _hg_in_kernel(h_ref, g_ref, w_ref, lb_ref, q_ref, kf_ref, lff_ref, kb_ref, lfb_ref, i_ref, gs_ref,
                  *, seqs, tm):
    hn = _rms(h_ref[...], g_ref[...]).astype(BF16)
    proj = lambda j: jnp.dot(hn, w_ref[:, j * D_MODEL:(j + 1) * D_MODEL], preferred_element_type=F32)
    valid = _valid_rows(pl.program_id(0) * tm, tm, seqs).astype(F32)
    lb = lb_ref[...]
    q = proj(0)
    q_ref[...] = q * jax.nn.sigmoid(q)
    for j, k_ref, lf_ref in ((1, kf_ref, lff_ref), (2, kb_ref, lfb_ref)):
        sg = jax.nn.sigmoid(proj(j))
        lf_ref[...] = jnp.log(lb + (1.0 - lb) * sg) * valid
        k_ref[...] = (1.0 - lb) * (1.0 - sg) * valid
    i_ref[...] = proj(3)
    g = proj(4)
    gs_ref[...] = g * jax.nn.sigmoid(g)


def _hg_chunks(rounds, masks):
    for bodies in rounds:
        for b in bodies:
            m = masks[b['dir']]
            cum = b['cum']
            c_ref = cum[m['mid']:m['mid'] + 1]
            c_last = cum[m['last']:m['last'] + 1]
            q, k = b['q'], b['k']
            b.update(q_in=q * jnp.exp(cum - c_ref), k_in=k * jnp.exp(c_ref - cum), q_st=q * jnp.exp(cum),
                     k_st=k * jnp.exp(c_last - cum), g=jnp.exp(c_last))
        for b in bodies:
            b['scores'] = jnp.where(masks[b['dir']]['incl'], _bdot_nt(b['q_in'], b['k_in']), 0.0)
    for bodies in rounds:
        for b in bodies:
            b['st'] = b['st_ref'][b['idx']]
            b['o_ref'][b['rows'], b['sl']] = _bdot(b['scores'], b['iv']) + _bdot_nt(b['q_st'], b['st'])
        for b in bodies:
            b['st_ref'][b['idx']] = b['st'] * b['g'] + _bdot(b['iv'].T, b['k_st'])


def _hg_scan_kernel(q_ref, k_ref, lf_ref, i_ref, *rest, direction, starts, chunks, seqs):
    if direction == 'f':
        ob_ref, gs_ref, h_ref, on_ref, w_ref, out_ref, st_ref, o_ref = rest
    else:
        o_ref, st_ref = rest
    c = pl.program_id(0)

    @pl.when(functools.reduce(jnp.logical_or, [c == s for s in starts]))
    def _():
        st_ref[...] = jnp.zeros(st_ref.shape, F32)

    t, s = _chunk_masks()
    if direction == 'f':
        masks = {'f': dict(incl=s <= t, mid=CHUNK // 2, last=CHUNK - 1)}
        tri = (s <= t).astype(BF16)
    else:
        masks = {'b': dict(incl=s >= t, mid=CHUNK // 2 - 1, last=0)}
        tri = (s >= t).astype(BF16)
    rounds = []
    for j in range(chunks):
        jj = j if direction == 'f' else chunks - 1 - j
        rows = slice(jj * CHUNK, (jj + 1) * CHUNK)
        cum = _dot_exact_rhs(tri, lf_ref[rows, :])
        rounds.append([dict(dir=direction, q=q_ref[rows, sl], k=k_ref[rows, sl], iv=i_ref[rows, sl], cum=cum[:, sl],
                            st_ref=st_ref, idx=hd, o_ref=o_ref, rows=rows, sl=sl)
                       for hd, sl in enumerate(slice(i * HG_HEAD_DIM, (i + 1) * HG_HEAD_DIM)
                                               for i in range(HG_HEADS))])
    _hg_chunks(rounds, masks)
    if direction == 'f':
        span = chunks * CHUNK
        o = o_ref[...] + ob_ref[...]
        parts = []
        for hd in range(HG_HEADS):
            oh = o[:, hd * HG_HEAD_DIM:(hd + 1) * HG_HEAD_DIM]
            parts.append(oh * lax.rsqrt(jnp.mean(oh * oh, axis=-1, keepdims=True) + NORM_EPS))
        on = jnp.concatenate(parts, axis=1) * on_ref[...] * gs_ref[...]
        y = h_ref[...] + _bdot(on, w_ref[...])
        out_ref[...] = jnp.where(_valid_rows(c * span, span, seqs), y, 0.0)


def _hgrn2_layer(h, norm_g, w_in, w_out, onorm_g, lb, seqs, tm):
    n_rows = h.shape[0]
    n_tiles = n_rows // tm
    n_chunks = n_rows // CHUNK
    row = _row_spec(tm)
    vec = _const_spec((1, D_MODEL))
    flat = jax.ShapeDtypeStruct((n_rows, D_MODEL), F32)
    q, kf, lff, kb, lfb, iv, gs = pl.pallas_call(
        functools.partial(_hg_in_kernel, seqs=seqs, tm=tm),
        grid=(n_tiles,),
        in_specs=[row, vec, _const_spec((D_MODEL, 5 * D_MODEL)), vec],
        out_specs=[row] * 7,
        out_shape=[flat] * 7,
        compiler_params=_params("parallel"),
        name="hg_in",
    )(h, norm_g.reshape(1, -1), w_in.astype(BF16), lb.reshape(1, -1))

    span = HG_CHUNKS_PER_STEP * CHUNK
    n_steps = n_rows // span
    fwd_starts = tuple(s // span for s, _, _, _ in seqs)
    bwd_starts = tuple(n_steps - (s + p) // span for s, _, _, p in seqs)
    cf = pl.BlockSpec((span, D_MODEL), lambda c: (c, 0))
    cb = pl.BlockSpec((span, D_MODEL), lambda c: (n_steps - 1 - c, 0))
    state = pltpu.VMEM((HG_HEADS, HG_HEAD_DIM, HG_HEAD_DIM), F32)
    scan = functools.partial(_hg_scan_kernel, chunks=HG_CHUNKS_PER_STEP, seqs=seqs)
    o_b = pl.pallas_call(
        functools.partial(scan, direction='b', starts=bwd_starts),
        grid=(n_steps,),
        in_specs=[cb] * 4,
        out_specs=cb,
        out_shape=flat,
        scratch_shapes=[state],
        compiler_params=_params("arbitrary"),
        name="hg_scan_b",
    )(q, kb, lfb, iv)
    return pl.pallas_call(
        functools.partial(scan, direction='f', starts=fwd_starts),
        grid=(n_steps,),
        in_specs=[cf] * 7 + [vec, _const_spec((D_MODEL, D_MODEL))],
        out_specs=cf,
        out_shape=flat,
        scratch_shapes=[state, pltpu.VMEM((span, D_MODEL), F32)],
        compiler_params=_params("arbitrary"),
        name="hg_scan_f",
    )(q, kf, lff, iv, o_b, gs, h, onorm_g.reshape(1, -1), w_out.astype(BF16))


def _rw_in_kernel(hp_ref, h_ref, hn_ref, g_ref, mu_ref, wr_ref, wk_ref, wv_ref, g1_ref, g2_ref,
                  w0_ref, w1_ref, w2_ref, a0_ref, a1_ref, a2_ref, kk_ref, ka_ref, rk_ref, e_ref, et_ref,
                  r_out, v_out, kkn_out, g_out, bonus_out,
                  lwf_out, kdf_out, abf_out, lwb_out, kdb_out, abb_out, *, seqs, tm, n_tiles):
    xe = _rms(_with_halo(hp_ref, h_ref, hn_ref, n_tiles), g_ref[...])
    xm1, x, xp1 = _shift_rows(xe, tm)
    xx = 0.5 * (xm1 + xp1) - x
    mu = mu_ref[...]
    xr, xw, xk, xv, xa, xg = (x + xx * mu[j:j + 1] for j in range(6))
    r = _bdot(xr, wr_ref[...])
    k = _bdot(xk, wk_ref[...])
    v = _bdot(xv, wv_ref[...])
    g_out[...] = _bdot(jax.nn.sigmoid(_bdot(xg, g1_ref[...])), g2_ref[...])
    z = w0_ref[...] + _bdot(jnp.tanh(_bdot(xw, w1_ref[...])), w2_ref[...])
    lw = -RW_DECAY_SCALE * jax.nn.sigmoid(z)
    a = jax.nn.sigmoid(a0_ref[...] + _bdot(_bdot(xa, a1_ref[...]), a2_ref[...]))
    kkr = k * kk_ref[...]
    ssq = _head_sum(kkr * kkr, e_ref, et_ref)
    kkn = kkr * lax.rsqrt(jnp.maximum(ssq, 1e-24))
    valid = _valid_rows(pl.program_id(0) * tm, tm, seqs).astype(F32)
    ka = ka_ref[...]
    kd_sum = None
    for d, (lw_o, kd_o, ab_o) in enumerate(((lwf_out, kdf_out, abf_out), (lwb_out, kdb_out, abb_out))):
        a_d = a[:, d * D_MODEL:(d + 1) * D_MODEL]
        kd = k * (1.0 + (a_d - 1.0) * ka)
        kd_sum = kd if kd_sum is None else kd_sum + kd
        lw_o[...] = lw[:, d * D_MODEL:(d + 1) * D_MODEL]
        kd_o[...] = kd * valid
        ab_o[...] = kkn * a_d
    r_out[...] = r
    v_out[...] = v
    kkn_out[...] = kkn
    bonus_out[...] = _head_sum(r * kd_sum * rk_ref[...], e_ref, et_ref) * v


class _HalfRows:
    def __init__(self, w, n, lower):
        self.w, self.lower, self.blocks = w, lower, n // (2 * w)
        self.whole = w < HALO

    def pick(self, x):
        if self.whole:
            return x
        off = self.w if self.lower else 0
        return jnp.concatenate([x[2 * j * self.w + off:2 * j * self.w + off + self.w]
                                for j in range(self.blocks)], axis=0)

    def spread(self, y):
        if self.whole:
            return y
        zero = jnp.zeros((self.w, y.shape[1]), y.dtype)
        parts = []
        for j in range(self.blocks):
            piece = y[j * self.w:(j + 1) * self.w]
            parts += [zero, piece] if self.lower else [piece, zero]
        return jnp.concatenate(parts, axis=0)


def _rw_prepare(bodies, masks, eye, lane_a):
    stack = lambda z_: jnp.concatenate([jnp.where(lane_a, z_, 0.0), jnp.where(lane_a, 0.0, z_)], axis=0)
    cast = lambda z_: z_.astype(BF16)
    dot = lambda x_, y_: jnp.dot(x_, y_, preferred_element_type=F32)
    n = 2 * CHUNK
    for b in bodies:
        m = masks[b['dir']]
        c, lw = b['c'], b['lw']
        e = c - lw
        c_mid = c[m['mid']:m['mid'] + 1]
        c_last = c[m['last']:m['last'] + 1]
        beta = -b['ab']
        dec_out = jnp.exp(c_mid - c)
        dec_end = jnp.exp(c_last - c)
        vs = stack(b['v'])
        cols = jnp.concatenate([beta * dec_out] * 2 + [b['k'] * dec_out] * 2, axis=0)
        b.update(
            lhs=cast(jnp.concatenate([stack(b['al'] * jnp.exp(e - c_mid)), stack(b['r'] * jnp.exp(c - c_mid))],
                                     axis=0)),
            rhs=cast(cols.T),
            vs=cast(vs), vs_t=cast(vs.T), al_h=cast(stack(b['al'] * jnp.exp(e))),
            r_h=cast(stack(b['r'] * jnp.exp(c))), b_h=cast(stack(beta * dec_end)),
            k_h=cast(stack(b['k'] * dec_end)), g=jnp.exp(c_last))
    for b in bodies:
        b['sc'] = dot(b['lhs'], b['rhs'])
    for b in bodies:
        m = masks[b['dir']]
        sc_a, sc_r = b['sc'][:n], b['sc'][n:]
        b.update(
            a_d=jnp.where(m['diag'], sc_a[:, :n], 0.0),
            a_off=[jnp.where(mo, sc_a[:, :n], 0.0) for mo in m['off']],
            a_ak=cast(jnp.where(m['strict'], sc_a[:, n:], 0.0)),
            a_rb=cast(jnp.where(m['incl'], sc_r[:, :n], 0.0)), a_rk=cast(jnp.where(m['incl'], sc_r[:, n:], 0.0)))
    for b in bodies:
        b['pw'] = cast(b['a_d'])
        b['inv'] = eye + b['a_d']
    for _ in range(RW_INV_BASE.bit_length() - 2):
        for b in bodies:
            b['pw'] = cast(dot(b['pw'], b['pw']))
        for b in bodies:
            b['inv'] = b['inv'] + dot(cast(b['inv']), b['pw'])
    for level, w in enumerate(masks['widths']):
        for b in bodies:
            b['half'] = _HalfRows(w, n, lower=b['dir'] == 'f')
            b['t'] = dot(cast(b['half'].pick(b['a_off'][level])), cast(b['inv']))
        for b in bodies:
            upd = dot(cast(b['half'].pick(b['inv'])), cast(b['half'].spread(b['t'])))
            b['inv'] = b['inv'] + b['half'].spread(upd)
    for b in bodies:
        b['inv16'] = cast(b['inv'])


def _rw_advance(bodies):
    for b in bodies:
        b['ht'] = b['ht_ref'][b['idx']]
        b['h16'] = b['ht'].T.astype(BF16)
        b['x'] = jnp.dot(jnp.concatenate([b['al_h'], b['a_ak']], axis=1),
                         jnp.concatenate([b['h16'], b['vs']], axis=0), preferred_element_type=F32)
    for b in bodies:
        b['u'] = jnp.dot(b['inv16'], b['x'].astype(BF16), preferred_element_type=F32)
    for b in bodies:
        y = jnp.dot(jnp.concatenate([b['r_h'], b['a_rb'], b['a_rk']], axis=1),
                    jnp.concatenate([b['h16'], b['u'].astype(BF16), b['vs']], axis=0),
                    preferred_element_type=F32)
        b['y_ref'][b['rows'], b['sl']] = y[:CHUNK] + y[CHUNK:]
    for b in bodies:
        b['ht_ref'][b['idx']] = b['ht'] * b['g'] + jnp.dot(
            jnp.concatenate([b['u'].T.astype(BF16), b['vs_t']], axis=1),
            jnp.concatenate([b['b_h'], b['k_h']], axis=0), preferred_element_type=F32)


def _rw_scan_kernel(rf_ref, kf_ref, vf_ref, alf_ref, abf_ref, lwf_ref,
                    rb_ref, kb_ref, vb_ref, alb_ref, abb_ref, lwb_ref,
                    yf_ref, yb_ref, hf_ref, hb_ref, *, fwd_starts, bwd_starts, pairs, chunks):
    c = pl.program_id(1)

    def reset_at(ref, steps):
        hit = functools.reduce(jnp.logical_or, [c == s for s in steps])

        @pl.when(hit)
        def _():
            ref[...] = jnp.zeros(ref.shape, F32)

    reset_at(hf_ref, fwd_starts)
    reset_at(hb_ref, bwd_starts)
    t, s = _chunk_masks()
    n = 2 * CHUNK
    ti = lax.broadcasted_iota(jnp.int32, (n, n), 0)
    si = lax.broadcasted_iota(jnp.int32, (n, n), 1)
    same = lambda width: (ti // width) == (si // width)
    eye = (ti == si).astype(F32)
    lane_a = lax.broadcasted_iota(jnp.int32, (CHUNK, LANES), 1) < RW_HEAD_DIM

    widths = [RW_INV_BASE << j for j in range((CHUNK // RW_INV_BASE).bit_length() - 1)]

    def dir_masks(before, before_incl, mid, last):
        return dict(mid=mid, last=last,
                    strict=same(CHUNK) & before, incl=same(CHUNK) & before_incl,
                    diag=same(RW_INV_BASE) & before,
                    off=[same(2 * w) & ~same(w) & before for w in widths])

    masks = {'f': dir_masks(si < ti, si <= ti, CHUNK // 2, CHUNK - 1),
             'b': dir_masks(si > ti, si >= ti, CHUNK // 2 - 1, 0), 'widths': widths}
    tri_f, tri_b = (s <= t).astype(BF16), (s >= t).astype(BF16)
    rounds = []
    for j in range(chunks):
        rf = slice(j * CHUNK, (j + 1) * CHUNK)
        rb = slice((chunks - 1 - j) * CHUNK, (chunks - j) * CHUNK)
        cum_f = _dot_exact_rhs(tri_f, lwf_ref[rf, :])
        cum_b = _dot_exact_rhs(tri_b, lwb_ref[rb, :])
        bodies = []
        for p in range(pairs):
            sl = slice(p * LANES, (p + 1) * LANES)
            bodies.append(dict(dir='f', r=rf_ref[rf, sl], k=kf_ref[rf, sl], v=vf_ref[rf, sl], al=alf_ref[rf, sl],
                               ab=abf_ref[rf, sl], lw=lwf_ref[rf, sl], c=cum_f[:, sl], ht_ref=hf_ref, idx=p,
                               y_ref=yf_ref, rows=rf, sl=sl))
            bodies.append(dict(dir='b', r=rb_ref[rb, sl], k=kb_ref[rb, sl], v=vb_ref[rb, sl], al=alb_ref[rb, sl],
                               ab=abb_ref[rb, sl], lw=lwb_ref[rb, sl], c=cum_b[:, sl], ht_ref=hb_ref, idx=p,
                               y_ref=yb_ref, rows=rb, sl=sl))
        _rw_prepare(bodies, masks, eye, lane_a)
        rounds.append(bodies)
    for bodies in rounds:
        _rw_advance(bodies)


def _rw_out_kernel(yf_ref, yb_ref, bonus_ref, g_ref, h_ref, gnw_ref, gnb_ref, e_ref, et_ref, w_ref, out_ref,
                   *, seqs, tm):
    y = yf_ref[...] + yb_ref[...]
    inv_n = 1.0 / RW_HEAD_DIM
    mean = _head_sum(y, e_ref, et_ref) * inv_n
    dlt = y - mean
    var = _head_sum(dlt * dlt, e_ref, et_ref) * inv_n
    yn = dlt * lax.rsqrt(var + RW_GN_EPS) * gnw_ref[...] + gnb_ref[...]
    out = (yn + bonus_ref[...]) * g_ref[...]
    res = h_ref[...] + _bdot(out, w_ref[...])
    valid = _valid_rows(pl.program_id(0) * tm, tm, seqs)
    out_ref[...] = jnp.where(valid, res, 0.0)


def _rwkv7_layer(h, norm_g, mu, w_r, w_k, w_v, w_o, w0, w1, w2, a0, a1, a2, g1, g2, k_k, k_a, r_k, gn_w, gn_b,
                 seqs, tm_in, tm_out, pairs):
    n_rows = h.shape[0]
    n_chunks = n_rows // CHUNK
    flat = jax.ShapeDtypeStruct((n_rows, D_MODEL), F32)
    vec = _const_spec((1, D_MODEL))
    vec2 = _const_spec((1, 2 * D_MODEL))
    row1 = lambda t: t.reshape(1, -1)

    def cat_dirs(w):
        return jnp.concatenate([w[0], w[1]], axis=1).astype(BF16)

    def blockdiag_dirs(w):
        z = jnp.zeros_like(w[0])
        return jnp.concatenate([jnp.concatenate([w[0], z], axis=1),
                                jnp.concatenate([z, w[1]], axis=1)], axis=0).astype(BF16)

    lora = g1.shape[1]
    g1p = jnp.pad(g1, ((0, 0), (0, RW_GATE_LORA_PAD - lora))).astype(BF16)
    g2p = jnp.pad(g2, ((0, RW_GATE_LORA_PAD - lora), (0, 0))).astype(BF16)
    w1c, w2c, a1c, a2c = cat_dirs(w1), blockdiag_dirs(w2), cat_dirs(a1), blockdiag_dirs(a2)
    e = (lax.broadcasted_iota(jnp.int32, (D_MODEL, LANES), 0) // RW_HEAD_DIM
         == lax.broadcasted_iota(jnp.int32, (D_MODEL, LANES), 1)).astype(BF16)
    et = e.T

    n_tiles = n_rows // tm_in
    prev, cur, nxt = _halo_specs(tm_in, n_rows)
    row = _row_spec(tm_in)
    sq = _const_spec((D_MODEL, D_MODEL))
    outs = pl.pallas_call(
        functools.partial(_rw_in_kernel, seqs=seqs, tm=tm_in, n_tiles=n_tiles),
        grid=(n_tiles,),
        in_specs=[prev, cur, nxt, vec, _const_spec((6, D_MODEL)), sq, sq, sq,
                  _const_spec(g1p.shape), _const_spec(g2p.shape),
                  vec2, _const_spec(w1c.shape), _const_spec(w2c.shape),
                  vec2, _const_spec(a1c.shape), _const_spec(a2c.shape),
                  vec, vec, vec, _const_spec(e.shape), _const_spec(et.shape)],
        out_specs=[row] * 11,
        out_shape=[flat] * 11,
        compiler_params=_params("parallel"),
        name="rw_in",
    )(h, h, h, row1(norm_g), mu, w_r.astype(BF16), w_k.astype(BF16), w_v.astype(BF16), g1p, g2p,
      row1(w0), w1c, w2c, row1(a0), a1c, a2c, row1(k_k), row1(k_a), row1(r_k), e, et)
    r, v, kkn, g, bonus, lwf, kdf, abf, lwb, kdb, abb = outs

    width = pairs * LANES
    n_groups = D_MODEL // width
    span = RW_CHUNKS_PER_STEP * CHUNK
    n_steps = n_rows // span
    fwd_starts = tuple(s // span for s, _, _, _ in seqs)
    bwd_starts = tuple(n_steps - (s + p) // span for s, _, _, p in seqs)
    cf = pl.BlockSpec((span, width), lambda p, c: (c, p))
    cb = pl.BlockSpec((span, width), lambda p, c: (n_steps - 1 - c, p))
    state = pltpu.VMEM((pairs, LANES, LANES), F32)
    y_f, y_b = pl.pallas_call(
        functools.partial(_rw_scan_kernel, fwd_starts=fwd_starts, bwd_starts=bwd_starts, pairs=pairs,
                          chunks=RW_CHUNKS_PER_STEP),
        grid=(n_groups, n_steps),
        in_specs=[cf] * 6 + [cb] * 6,
        out_specs=[cf, cb],
        out_shape=[flat, flat],
        scratch_shapes=[state, state],
        compiler_params=_params("parallel", "arbitrary"),
        name="rw_scan",
    )(r, kdf, v, kkn, abf, lwf, r, kdb, v, kkn, abb, lwb)

    row = _row_spec(tm_out)
    return pl.pallas_call(
        functools.partial(_rw_out_kernel, seqs=seqs, tm=tm_out),
        grid=(n_rows // tm_out,),
        in_specs=[row, row, row, row, row, vec, vec, _const_spec(e.shape), _const_spec(et.shape), sq],
        out_specs=row,
        out_shape=flat,
        compiler_params=_params("parallel"),
        name="rw_out",
    )(y_f, y_b, bonus, g, h, row1(gn_w), row1(gn_b), e, et, w_o.astype(BF16))


def _ffn_kernel(hp_ref, h_ref, hn_ref, g_ref, wu_ref, wv_ref, cw_ref, cb_ref, wo_ref, gf_ref, *out_refs,
                seqs, tm, n_tiles, group_tiles):
    he = _with_halo(hp_ref, h_ref, hn_ref, n_tiles)
    xe = _rms(he, g_ref[...])
    ue = _bdot(xe, wu_ref[...])
    v = _bdot(xe[HALO:HALO + tm], wv_ref[...])
    um1, u0, up1 = _shift_rows(ue, tm)
    cw = cw_ref[...]
    u = um1 * cw[0:1] + u0 * cw[1:2] + up1 * cw[2:3] + cb_ref[...]
    y = he[HALO:HALO + tm] + _bdot(jax.nn.gelu(u) * v, wo_ref[...])
    valid = _valid_rows(pl.program_id(0) * tm, tm, seqs)
    y = jnp.where(valid, y, 0.0)
    if group_tiles is None:
        out_refs[0][...] = y
        return
    y = _rms(y, gf_ref[...])
    i = pl.program_id(0)
    for out_ref, (lo, hi) in zip(out_refs, group_tiles):
        @pl.when((i >= lo) & (i <= hi))
        def _(out_ref=out_ref):
            out_ref[...] = y


def _ffn_layer(h, norm_g, w_in, conv_w, conv_b, w_out, final_g, seqs, tm, groups):
    n_rows = h.shape[0]
    n_tiles = n_rows // tm
    if groups is None:
        group_tiles = None
        out_specs = [_row_spec(tm)]
        out_shape = [jax.ShapeDtypeStruct((n_rows, D_MODEL), F32)]
    else:
        group_tiles, out_specs, out_shape = [], [], []
        for members in groups:
            fronts = [seqs[k][0] // tm for k in members]
            lo = fronts[0]
            hi = (seqs[members[-1]][0] + seqs[members[-1]][3]) // tm - 1
            n_blocks = hi - lo + 1 - len(members)

            def token_block(i, lo=lo, later=tuple(fronts[1:]), n_blocks=n_blocks):
                skipped = sum((i > f).astype(jnp.int32) for f in later) if later else 0
                return jnp.clip(i - lo - 1 - skipped, 0, n_blocks - 1), 0

            group_tiles.append((lo, hi))
            out_specs.append(pl.BlockSpec((tm, D_MODEL), token_block))
            out_shape.append(jax.ShapeDtypeStruct((n_blocks * tm, D_MODEL), F32))
    prev, cur, nxt = _halo_specs(tm, n_rows)
    vec = _const_spec((1, D_MODEL))
    vecf = _const_spec((1, D_FF))
    return pl.pallas_call(
        functools.partial(_ffn_kernel, seqs=seqs, tm=tm, n_tiles=n_tiles, group_tiles=group_tiles),
        grid=(n_tiles,),
        in_specs=[prev, cur, nxt, vec, _const_spec((D_MODEL, D_FF)), _const_spec((D_MODEL, D_FF)),
                  _const_spec((3, D_FF)), vecf, _const_spec((D_FF, D_MODEL)), vec],
        out_specs=out_specs,
        out_shape=out_shape,
        compiler_params=_params("arbitrary"),
        name="ffn",
    )(h, h, h, norm_g.reshape(1, -1), w_in[:, :D_FF].astype(BF16), w_in[:, D_FF:].astype(BF16),
      conv_w, conv_b.reshape(1, -1), w_out.astype(BF16), final_g.reshape(1, -1))


def _layout(token_counts):
    seqs, start = [], 0
    for n in token_counts:
        padded = ROW_ALIGN + -(-n // ROW_ALIGN) * ROW_ALIGN
        seqs.append((start, start + ROW_ALIGN - N_META, N_META + n, padded))
        start += padded
    return tuple(seqs), start


def _trunk(xs, p, groups=None, tm=ROW_ALIGN, tm_rw_in=ROW_ALIGN, rw_pairs=8):
    assert all(x.shape[0] % ROW_ALIGN == 0 for x in xs)
    groups = [[k] for k in range(len(xs))] if groups is None else groups
    seqs, n_rows = _layout([x.shape[0] for x in xs])
    meta = p['meta_tokens'].astype(F32)
    pieces = []
    for x in xs:
        pieces += [jnp.zeros((ROW_ALIGN - N_META, D_MODEL), F32), meta, x]
    h = jnp.concatenate(pieces, axis=0)
    lb_all = jnp.cumsum(jax.nn.softmax(p['hg_lower_bound'].astype(F32), axis=0), axis=0)
    depth = p['norm_mix'].shape[0]
    for layer in range(depth):
        j = layer // N_MIXERS
        if layer % N_MIXERS == 0:
            h = _hgrn2_layer(h, p['norm_mix'][layer], p['hg_w_in'][j], p['hg_w_out'][j], p['hg_onorm'][j],
                             lb_all[j], seqs, tm)
        else:
            h = _rwkv7_layer(h, p['norm_mix'][layer], p['rw_mu'][j], p['rw_w_r'][j], p['rw_w_k'][j],
                             p['rw_w_v'][j], p['rw_w_o'][j], p['rw_w0'][j], p['rw_w1'][j], p['rw_w2'][j],
                             p['rw_a0'][j], p['rw_a1'][j], p['rw_a2'][j], p['rw_g1'][j], p['rw_g2'][j],
                             p['rw_k_k'][j], p['rw_k_a'][j], p['rw_r_k'][j], p['rw_gn_w'][j], p['rw_gn_b'][j],
                             seqs, tm_rw_in, tm, rw_pairs)
        h = _ffn_layer(h, p['norm_ffn'][layer], p['ffn_w_in'][layer], p['ffn_conv_w'][layer],
                       p['ffn_conv_b'][layer], p['ffn_w_out'][layer], p['norm_final'], seqs, tm,
                       groups if layer == depth - 1 else None)
        h = h if layer == depth - 1 else h[0]
    return h


def kernel(x_prompt, x_sample, meta_tokens, norm_mix, norm_ffn, norm_final, hg_w_in, hg_w_out, hg_lower_bound,
           hg_onorm, rw_mu, rw_w_r, rw_w_k, rw_w_v, rw_w_o, rw_w0, rw_w1, rw_w2, rw_a0, rw_a1, rw_a2, rw_g1,
           rw_g2, rw_k_k, rw_k_a, rw_r_k, rw_gn_w, rw_gn_b, ffn_w_in, ffn_conv_w, ffn_conv_b, ffn_w_out):
    p = {
        'meta_tokens': meta_tokens, 'norm_mix': norm_mix, 'norm_ffn': norm_ffn, 'norm_final': norm_final,
        'hg_w_in': hg_w_in, 'hg_w_out': hg_w_out, 'hg_lower_bound': hg_lower_bound, 'hg_onorm': hg_onorm,
        'rw_mu': rw_mu, 'rw_w_r': rw_w_r, 'rw_w_k': rw_w_k, 'rw_w_v': rw_w_v, 'rw_w_o': rw_w_o,
        'rw_w0': rw_w0, 'rw_w1': rw_w1, 'rw_w2': rw_w2, 'rw_a0': rw_a0, 'rw_a1': rw_a1, 'rw_a2': rw_a2,
        'rw_g1': rw_g1, 'rw_g2': rw_g2, 'rw_k_k': rw_k_k, 'rw_k_a': rw_k_a, 'rw_r_k': rw_r_k,
        'rw_gn_w': rw_gn_w, 'rw_gn_b': rw_gn_b,
        'ffn_w_in': ffn_w_in, 'ffn_conv_w': ffn_conv_w, 'ffn_conv_b': ffn_conv_b, 'ffn_w_out': ffn_w_out,
    }
    xs = [x_prompt[b] for b in range(x_prompt.shape[0])] + [x_sample[b] for b in range(x_sample.shape[0])]
    n_p, n_s = x_prompt.shape[0], x_sample.shape[0]
    y_p, y_s = _trunk(xs, p, groups=[list(range(n_p)), list(range(n_p, n_p + n_s))])
    return (y_p.reshape(x_prompt.shape), y_s.reshape(x_sample.shape))
```

```python
import functools

import jax
import jax.numpy as jnp
from jax import lax
from jax.experimental import pallas as pl
from jax.experimental.pallas import tpu as pltpu

F32 = jnp.float32
BF16 = jnp.bfloat16
HIGHEST = lax.Precision.HIGHEST

D_MODEL = 1024
D_FF = 2816
N_META = 16
N_MIXERS = 2
HG_HEADS = 8
HG_HEAD_DIM = 128
RW_HEADS = 16
RW_HEAD_DIM = 64
RW_GATE_LORA_PAD = 256
NORM_EPS = 1e-6
RW_GN_EPS = 64e-5
RW_DECAY_SCALE = 0.6065306597126334

LANES = 128
HALO = 8
CHUNK = 64
HG_CHUNKS_PER_STEP = 4
RW_CHUNKS_PER_STEP = 4
ROW_ALIGN = 256
VMEM_LIMIT_BYTES = 56 * 1024 * 1024

NT_DIMS = (((1,), (1,)), ((), ()))


def _const_spec(shape):
    nd = len(shape)
    return pl.BlockSpec(shape, lambda *_: (0,) * nd, pipeline_mode=pl.Buffered(1))


def _params(*sem):
    return pltpu.CompilerParams(dimension_semantics=sem, vmem_limit_bytes=VMEM_LIMIT_BYTES)


def _valid_rows(row0, n, seqs):
    r = row0 + lax.broadcasted_iota(jnp.int32, (n, 1), 0)
    m = None
    for _, first, length, _ in seqs:
        t = (r >= first) & (r < first + length)
        m = t if m is None else (m | t)
    return m


def _rms(x, g):
    return x * lax.rsqrt(jnp.mean(x * x, axis=-1, keepdims=True) + NORM_EPS) * g


def _bdot(a, b):
    return jnp.dot(a.astype(BF16), b.astype(BF16), preferred_element_type=F32)


def _bdot_nt(a, b):
    return lax.dot_general(a.astype(BF16), b.astype(BF16), NT_DIMS, preferred_element_type=F32)


RW_INV_BASE = 4


def _split(x, terms):
    hi = x.astype(BF16)
    if terms == 1:
        return (hi,)
    return hi, (x - hi.astype(F32)).astype(BF16)


def _split3(x):
    x1 = x.astype(BF16)
    r1 = x - x1.astype(F32)
    x2 = r1.astype(BF16)
    x3 = (r1 - x2.astype(F32)).astype(BF16)
    return x1, x2, x3


def _dot_exact_rhs(a_bf16, x):
    x1, x2, x3 = _split3(x)
    d = lambda t: jnp.dot(a_bf16, t, preferred_element_type=F32)
    return d(x1) + d(x2) + d(x3)


def _head_sum(x, e_ref, et_ref):
    def dot2(z, ind):
        z1, z2 = _split(z, 2)
        return (jnp.dot(z1, ind, preferred_element_type=F32) + jnp.dot(z2, ind, preferred_element_type=F32))

    s = dot2(x, e_ref[...])
    return dot2(s, et_ref[...])


def _shift_rows(xe, tm):
    n = xe.shape[0]
    prev = pltpu.roll(xe, 1, axis=0)[HALO:HALO + tm]
    nxt = pltpu.roll(xe, n - 1, axis=0)[HALO:HALO + tm]
    return prev, xe[HALO:HALO + tm], nxt


def _with_halo(hp_ref, h_ref, hn_ref, n_tiles):
    i = pl.program_id(0)
    hp = jnp.where(i > 0, hp_ref[...], 0.0)
    hn = jnp.where(i < n_tiles - 1, hn_ref[...], 0.0)
    return jnp.concatenate([hp, h_ref[...], hn], axis=0)


def _halo_specs(tm, n_rows):
    r = tm // HALO
    last = n_rows // HALO - 1
    prev = pl.BlockSpec((HALO, D_MODEL), lambda i: (jnp.maximum(i * r - 1, 0), 0))
    cur = pl.BlockSpec((tm, D_MODEL), lambda i: (i, 0))
    nxt = pl.BlockSpec((HALO, D_MODEL), lambda i: (jnp.minimum((i + 1) * r, last), 0))
    return prev, cur, nxt


def _row_spec(tm, width=D_MODEL):
    return pl.BlockSpec((tm, width), lambda i: (i, 0))


def _chunk_masks():
    t = lax.broadcasted_iota(jnp.int32, (CHUNK, CHUNK), 0)
    s = lax.broadcasted_iota(jnp.int32, (CHUNK, CHUNK), 1)
    return t, s


def _hg_in_kernel(h_ref, g_ref, w_ref, lb_ref, q_ref, kf_ref, lff_ref, kb_ref, lfb_ref, i_ref, gs_ref,
                  *, seqs, tm):
    hn = _rms(h_ref[...], g_ref[...]).astype(BF16)
    proj = lambda j: jnp.dot(hn, w_ref[:, j * D_MODEL:(j + 1) * D_MODEL], preferred_element_type=F32)
    valid = _valid_rows(pl.program_id(0) * tm, tm, seqs).astype(F32)
    lb = lb_ref[...]
    q = proj(0)
    q_ref[...] = q * jax.nn.sigmoid(q)
    for j, k_ref, lf_ref in ((1, kf_ref, lff_ref), (2, kb_ref, lfb_ref)):
        sg = jax.nn.sigmoid(proj(j))
        lf_ref[...] = jnp.log(lb + (1.0 - lb) * sg) * valid
        k_ref[...] = (1.0 - lb) * (1.0 - sg) * valid
    i_ref[...] = proj(3)
    g = proj(4)
    gs_ref[...] = g * jax.nn.sigmoid(g)


def _hg_chunks(rounds, masks):
    for bodies in rounds:
        for b in bodies:
            m = masks[b['dir']]
            cum = b['cum']
            c_ref = cum[m['mid']:m['mid'] + 1]
            c_last = cum[m['last']:m['last'] + 1]
            q, k = b['q'], b['k']
            b.update(q_in=q * jnp.exp(cum - c_ref), k_in=k * jnp.exp(c_ref - cum), q_st=q * jnp.exp(cum),
                     k_st=k * jnp.exp(c_last - cum), g=jnp.exp(c_last))
        for b in bodies:
            b['scores'] = jnp.where(masks[b['dir']]['incl'], _bdot_nt(b['q_in'], b['k_in']), 0.0)
    for bodies in rounds:
        for b in bodies:
            b['st'] = b['st_ref'][b['idx']]
            b['o_ref'][b['rows'], b['sl']] = _bdot(b['scores'], b['iv']) + _bdot_nt(b['q_st'], b['st'])
        for b in bodies:
            b['st_ref'][b['idx']] = b['st'] * b['g'] + _bdot(b['iv'].T, b['k_st'])


def _hg_scan_kernel(q_ref, k_ref, lf_ref, i_ref, *rest, direction, starts, chunks, seqs):
    if direction == 'f':
        ob_ref, gs_ref, h_ref, on_ref, w_ref, out_ref, st_ref, o_ref = rest
    else:
        o_ref, st_ref = rest
    c = pl.program_id(0)

    @pl.when(functools.reduce(jnp.logical_or, [c == s for s in starts]))
    def _():
        st_ref[...] = jnp.zeros(st_ref.shape, F32)

    t, s = _chunk_masks()
    if direction == 'f':
        masks = {'f': dict(incl=s <= t, mid=CHUNK // 2, last=CHUNK - 1)}
        tri = (s <= t).astype(BF16)
    else:
        masks = {'b': dict(incl=s >= t, mid=CHUNK // 2 - 1, last=0)}
        tri = (s >= t).astype(BF16)
    rounds = []
    for j in range(chunks):
        jj = j if direction == 'f' else chunks - 1 - j
        rows = slice(jj * CHUNK, (jj + 1) * CHUNK)
        cum = _dot_exact_rhs(tri, lf_ref[rows, :])
        rounds.append([dict(dir=direction, q=q_ref[rows, sl], k=k_ref[rows, sl], iv=i_ref[rows, sl], cum=cum[:, sl],
                            st_ref=st_ref, idx=hd, o_ref=o_ref, rows=rows, sl=sl)
                       for hd, sl in enumerate(slice(i * HG_HEAD_DIM, (i + 1) * HG_HEAD_DIM)
                                               for i in range(HG_HEADS))])
    _hg_chunks(rounds, masks)
    if direction == 'f':
        span = chunks * CHUNK
        o = o_ref[...] + ob_ref[...]
        parts = []
        for hd in range(HG_HEADS):
            oh = o[:, hd * HG_HEAD_DIM:(hd + 1) * HG_HEAD_DIM]
            parts.append(oh * lax.rsqrt(jnp.mean(oh * oh, axis=-1, keepdims=True) + NORM_EPS))
        on = jnp.concatenate(parts, axis=1) * on_ref[...] * gs_ref[...]
        y = h_ref[...] + _bdot(on, w_ref[...])
        out_ref[...] = jnp.where(_valid_rows(c * span, span, seqs), y, 0.0)


def _hgrn2_layer(h, norm_g, w_in, w_out, onorm_g, lb, seqs, tm):
    n_rows = h.shape[0]
    n_tiles = n_rows // tm
    n_chunks = n_rows // CHUNK
    row = _row_spec(tm)
    vec = _const_spec((1, D_MODEL))
    flat = jax.ShapeDtypeStruct((n_rows, D_MODEL), F32)
    q, kf, lff, kb, lfb, iv, gs = pl.pallas_call(
        functools.partial(_hg_in_kernel, seqs=seqs, tm=tm),
        grid=(n_tiles,),
        in_specs=[row, vec, _const_spec((D_MODEL, 5 * D_MODEL)), vec],
        out_specs=[row] * 7,
        out_shape=[flat] * 7,
        compiler_params=_params("parallel"),
        name="hg_in",
    )(h, norm_g.reshape(1, -1), w_in.astype(BF16), lb.reshape(1, -1))

    span = HG_CHUNKS_PER_STEP * CHUNK
    n_steps = n_rows // span
    fwd_starts = tuple(s // span for s, _, _, _ in seqs)
    bwd_starts = tuple(n_steps - (s + p) // span for s, _, _, p in seqs)
    cf = pl.BlockSpec((span, D_MODEL), lambda c: (c, 0))
    cb = pl.BlockSpec((span, D_MODEL), lambda c: (n_steps - 1 - c, 0))
    state = pltpu.VMEM((HG_HEADS, HG_HEAD_DIM, HG_HEAD_DIM), F32)
    scan = functools.partial(_hg_scan_kernel, chunks=HG_CHUNKS_PER_STEP, seqs=seqs)
    o_b = pl.pallas_call(
        functools.partial(scan, direction='b', starts=bwd_starts),
        grid=(n_steps,),
        in_specs=[cb] * 4,
        out_specs=cb,
        out_shape=flat,
        scratch_shapes=[state],
        compiler_params=_params("arbitrary"),
        name="hg_scan_b",
    )(q, kb, lfb, iv)
    return pl.pallas_call(
        functools.partial(scan, direction='f', starts=fwd_starts),
        grid=(n_steps,),
        in_specs=[cf] * 7 + [vec, _const_spec((D_MODEL, D_MODEL))],
        out_specs=cf,
        out_shape=flat,
        scratch_shapes=[state, pltpu.VMEM((span, D_MODEL), F32)],
        compiler_params=_params("arbitrary"),
        name="hg_scan_f",
    )(q, kf, lff, iv, o_b, gs, h, onorm_g.reshape(1, -1), w_out.astype(BF16))


def _rw_in_kernel(hp_ref, h_ref, hn_ref, g_ref, mu_ref, wr_ref, wk_ref, wv_ref, g1_ref, g2_ref,
                  w0_ref, w1_ref, w2_ref, a0_ref, a1_ref, a2_ref, kk_ref, ka_ref, rk_ref, e_ref, et_ref,
                  r_out, v_out, kkn_out, g_out, bonus_out,
                  lwf_out, kdf_out, abf_out, lwb_out, kdb_out, abb_out, *, seqs, tm, n_tiles):
    xe = _rms(_with_halo(hp_ref, h_ref, hn_ref, n_tiles), g_ref[...])
    xm1, x, xp1 = _shift_rows(xe, tm)
    xx = 0.5 * (xm1 + xp1) - x
    mu = mu_ref[...]
    xr, xw, xk, xv, xa, xg = (x + xx * mu[j:j + 1] for j in range(6))
    r = _bdot(xr, wr_ref[...])
    k = _bdot(xk, wk_ref[...])
    v = _bdot(xv, wv_ref[...])
    g_out[...] = _bdot(jax.nn.sigmoid(_bdot(xg, g1_ref[...])), g2_ref[...])
    z = w0_ref[...] + _bdot(jnp.tanh(_bdot(xw, w1_ref[...])), w2_ref[...])
    lw = -RW_DECAY_SCALE * jax.nn.sigmoid(z)
    a = jax.nn.sigmoid(a0_ref[...] + _bdot(_bdot(xa, a1_ref[...]), a2_ref[...]))
    kkr = k * kk_ref[...]
    ssq = _head_sum(kkr * kkr, e_ref, et_ref)
    kkn = kkr * lax.rsqrt(jnp.maximum(ssq, 1e-24))
    valid = _valid_rows(pl.program_id(0) * tm, tm, seqs).astype(F32)
    ka = ka_ref[...]
    kd_sum = None
    for d, (lw_o, kd_o, ab_o) in enumerate(((lwf_out, kdf_out, abf_out), (lwb_out, kdb_out, abb_out))):
        a_d = a[:, d * D_MODEL:(d + 1) * D_MODEL]
        kd = k * (1.0 + (a_d - 1.0) * ka)
        kd_sum = kd if kd_sum is None else kd_sum + kd
        lw_o[...] = lw[:, d * D_MODEL:(d + 1) * D_MODEL]
        kd_o[...] = kd * valid
        ab_o[...] = kkn * a_d
    r_out[...] = r
    v_out[...] = v
    kkn_out[...] = kkn
    bonus_out[...] = _head_sum(r * kd_sum * rk_ref[...], e_ref, et_ref) * v


class _HalfRows:
    def __init__(self, w, n, lower):
        self.w, self.lower, self.blocks = w, lower, n // (2 * w)
        self.whole = w < HALO

    def pick(self, x):
        if self.whole:
            return x
        off = self.w if self.lower else 0
        return jnp.concatenate([x[2 * j * self.w + off:2 * j * self.w + off + self.w]
                                for j in range(self.blocks)], axis=0)

    def spread(self, y):
        if self.whole:
            return y
        zero = jnp.zeros((self.w, y.shape[1]), y.dtype)
        parts = []
        for j in range(self.blocks):
            piece = y[j * self.w:(j + 1) * self.w]
            parts += [zero, piece] if self.lower else [piece, zero]
        return jnp.concatenate(parts, axis=0)


def _rw_prepare(bodies, masks, eye, lane_a):
    stack = lambda z_: jnp.concatenate([jnp.where(lane_a, z_, 0.0), jnp.where(lane_a, 0.0, z_)], axis=0)
    cast = lambda z_: z_.astype(BF16)
    dot = lambda x_, y_: jnp.dot(x_, y_, preferred_element_type=F32)
    n = 2 * CHUNK
    for b in bodies:
        m = masks[b['dir']]
        c, lw = b['c'], b['lw']
        e = c - lw
        c_mid = c[m['mid']:m['mid'] + 1]
        c_last = c[m['last']:m['last'] + 1]
        beta = -b['ab']
        dec_out = jnp.exp(c_mid - c)
        dec_end = jnp.exp(c_last - c)
        vs = stack(b['v'])
        cols = jnp.concatenate([beta * dec_out] * 2 + [b['k'] * dec_out] * 2, axis=0)
        b.update(
            lhs=cast(jnp.concatenate([stack(b['al'] * jnp.exp(e - c_mid)), stack(b['r'] * jnp.exp(c - c_mid))],
                                     axis=0)),
            rhs=cast(cols.T),
            vs=cast(vs), vs_t=cast(vs.T), al_h=cast(stack(b['al'] * jnp.exp(e))),
            r_h=cast(stack(b['r'] * jnp.exp(c))), b_h=cast(stack(beta * dec_end)),
            k_h=cast(stack(b['k'] * dec_end)), g=jnp.exp(c_last))
    for b in bodies:
        b['sc'] = dot(b['lhs'], b['rhs'])
    for b in bodies:
        m = masks[b['dir']]
        sc_a, sc_r = b['sc'][:n], b['sc'][n:]
        b.update(
            a_d=jnp.where(m['diag'], sc_a[:, :n], 0.0),
            a_off=[jnp.where(mo, sc_a[:, :n], 0.0) for mo in m['off']],
            a_ak=cast(jnp.where(m['strict'], sc_a[:, n:], 0.0)),
            a_rb=cast(jnp.where(m['incl'], sc_r[:, :n], 0.0)), a_rk=cast(jnp.where(m['incl'], sc_r[:, n:], 0.0)))
    for b in bodies:
        b['pw'] = cast(b['a_d'])
        b['inv'] = eye + b['a_d']
    for _ in range(RW_INV_BASE.bit_length() - 2):
        for b in bodies:
            b['pw'] = cast(dot(b['pw'], b['pw']))
        for b in bodies:
            b['inv'] = b['inv'] + dot(cast(b['inv']), b['pw'])
    for level, w in enumerate(masks['widths']):
        for b in bodies:
            b['half'] = _HalfRows(w, n, lower=b['dir'] == 'f')
            b['t'] = dot(cast(b['half'].pick(b['a_off'][level])), cast(b['inv']))
        for b in bodies:
            upd = dot(cast(b['half'].pick(b['inv'])), cast(b['half'].spread(b['t'])))
            b['inv'] = b['inv'] + b['half'].spread(upd)
    for b in bodies:
        b['inv16'] = cast(b['inv'])


def _rw_advance(bodies):
    for b in bodies:
        b['ht'] = b['ht_ref'][b['idx']]
        b['h16'] = b['ht'].T.astype(BF16)
        b['x'] = jnp.dot(jnp.concatenate([b['al_h'], b['a_ak']], axis=1),
                         jnp.concatenate([b['h16'], b['vs']], axis=0), preferred_element_type=F32)
    for b in bodies:
        b['u'] = jnp.dot(b['inv16'], b['x'].astype(BF16), preferred_element_type=F32)
    for b in bodies:
        y = jnp.dot(jnp.concatenate([b['r_h'], b['a_rb'], b['a_rk']], axis=1),
                    jnp.concatenate([b['h16'], b['u'].astype(BF16), b['vs']], axis=0),
                    preferred_element_type=F32)
        b['y_ref'][b['rows'], b['sl']] = y[:CHUNK] + y[CHUNK:]
    for b in bodies:
        b['ht_ref'][b['idx']] = b['ht'] * b['g'] + jnp.dot(
            jnp.concatenate([b['u'].T.astype(BF16), b['vs_t']], axis=1),
            jnp.concatenate([b['b_h'], b['k_h']], axis=0), preferred_element_type=F32)


def _rw_scan_kernel(rf_ref, kf_ref, vf_ref, alf_ref, abf_ref, lwf_ref,
                    rb_ref, kb_ref, vb_ref, alb_ref, abb_ref, lwb_ref,
                    yf_ref, yb_ref, hf_ref, hb_ref, *, fwd_starts, bwd_starts, pairs, chunks):
    c = pl.program_id(1)

    def reset_at(ref, steps):
        hit = functools.reduce(jnp.logical_or, [c == s for s in steps])

        @pl.when(hit)
        def _():
            ref[...] = jnp.zeros(ref.shape, F32)

    reset_at(hf_ref, fwd_starts)
    reset_at(hb_ref, bwd_starts)
    t, s = _chunk_masks()
    n = 2 * CHUNK
    ti = lax.broadcasted_iota(jnp.int32, (n, n), 0)
    si = lax.broadcasted_iota(jnp.int32, (n, n), 1)
    same = lambda width: (ti // width) == (si // width)
    eye = (ti == si).astype(F32)
    lane_a = lax.broadcasted_iota(jnp.int32, (CHUNK, LANES), 1) < RW_HEAD_DIM

    widths = [RW_INV_BASE << j for j in range((CHUNK // RW_INV_BASE).bit_length() - 1)]

    def dir_masks(before, before_incl, mid, last):
        return dict(mid=mid, last=last,
                    strict=same(CHUNK) & before, incl=same(CHUNK) & before_incl,
                    diag=same(RW_INV_BASE) & before,
                    off=[same(2 * w) & ~same(w) & before for w in widths])

    masks = {'f': dir_masks(si < ti, si <= ti, CHUNK // 2, CHUNK - 1),
             'b': dir_masks(si > ti, si >= ti, CHUNK // 2 - 1, 0), 'widths': widths}
    tri_f, tri_b = (s <= t).astype(BF16), (s >= t).astype(BF16)
    rounds = []
    for j in range(chunks):
        rf = slice(j * CHUNK, (j + 1) * CHUNK)
        rb = slice((chunks - 1 - j) * CHUNK, (chunks - j) * CHUNK)
        cum_f = _dot_exact_rhs(tri_f, lwf_ref[rf, :])
        cum_b = _dot_exact_rhs(tri_b, lwb_ref[rb, :])
        bodies = []
        for p in range(pairs):
            sl = slice(p * LANES, (p + 1) * LANES)
            bodies.append(dict(dir='f', r=rf_ref[rf, sl], k=kf_ref[rf, sl], v=vf_ref[rf, sl], al=alf_ref[rf, sl],
                               ab=abf_ref[rf, sl], lw=lwf_ref[rf, sl], c=cum_f[:, sl], ht_ref=hf_ref, idx=p,
                               y_ref=yf_ref, rows=rf, sl=sl))
            bodies.append(dict(dir='b', r=rb_ref[rb, sl], k=kb_ref[rb, sl], v=vb_ref[rb, sl], al=alb_ref[rb, sl],
                               ab=abb_ref[rb, sl], lw=lwb_ref[rb, sl], c=cum_b[:, sl], ht_ref=hb_ref, idx=p,
                               y_ref=yb_ref, rows=rb, sl=sl))
        _rw_prepare(bodies, masks, eye, lane_a)
        rounds.append(bodies)
    for bodies in rounds:
        _rw_advance(bodies)


def _rw_out_kernel(yf_ref, yb_ref, bonus_ref, g_ref, h_ref, gnw_ref, gnb_ref, e_ref, et_ref, w_ref, out_ref,
                   *, seqs, tm):
    y = yf_ref[...] + yb_ref[...]
    inv_n = 1.0 / RW_HEAD_DIM
    mean = _head_sum(y, e_ref, et_ref) * inv_n
    dlt = y - mean
    var = _head_sum(dlt * dlt, e_ref, et_ref) * inv_n
    yn = dlt * lax.rsqrt(var + RW_GN_EPS) * gnw_ref[...] + gnb_ref[...]
    out = (yn + bonus_ref[...]) * g_ref[...]
    res = h_ref[...] + _bdot(out, w_ref[...])
    valid = _valid_rows(pl.program_id(0) * tm, tm, seqs)
    out_ref[...] = jnp.where(valid, res, 0.0)


def _rwkv7_layer(h, norm_g, mu, w_r, w_k, w_v, w_o, w0, w1, w2, a0, a1, a2, g1, g2, k_k, k_a, r_k, gn_w, gn_b,
                 seqs, tm_in, tm_out, pairs):
    n_rows = h.shape[0]
    n_chunks = n_rows // CHUNK
    flat = jax.ShapeDtypeStruct((n_rows, D_MODEL), F32)
    vec = _const_spec((1, D_MODEL))
    vec2 = _const_spec((1, 2 * D_MODEL))
    row1 = lambda t: t.reshape(1, -1)

    def cat_dirs(w):
        return jnp.concatenate([w[0], w[1]], axis=1).astype(BF16)

    def blockdiag_dirs(w):
        z = jnp.zeros_like(w[0])
        return jnp.concatenate([jnp.concatenate([w[0], z], axis=1),
                                jnp.concatenate([z, w[1]], axis=1)], axis=0).astype(BF16)

    lora = g1.shape[1]
    g1p = jnp.pad(g1, ((0, 0), (0, RW_GATE_LORA_PAD - lora))).astype(BF16)
    g2p = jnp.pad(g2, ((0, RW_GATE_LORA_PAD - lora), (0, 0))).astype(BF16)
    w1c, w2c, a1c, a2c = cat_dirs(w1), blockdiag_dirs(w2), cat_dirs(a1), blockdiag_dirs(a2)
    e = (lax.broadcasted_iota(jnp.int32, (D_MODEL, LANES), 0) // RW_HEAD_DIM
         == lax.broadcasted_iota(jnp.int32, (D_MODEL, LANES), 1)).astype(BF16)
    et = e.T

    n_tiles = n_rows // tm_in
    prev, cur, nxt = _halo_specs(tm_in, n_rows)
    row = _row_spec(tm_in)
    sq = _const_spec((D_MODEL, D_MODEL))
    outs = pl.pallas_call(
        functools.partial(_rw_in_kernel, seqs=seqs, tm=tm_in, n_tiles=n_tiles),
        grid=(n_tiles,),
        in_specs=[prev, cur, nxt, vec, _const_spec((6, D_MODEL)), sq, sq, sq,
                  _const_spec(g1p.shape), _const_spec(g2p.shape),
                  vec2, _const_spec(w1c.shape), _const_spec(w2c.shape),
                  vec2, _const_spec(a1c.shape), _const_spec(a2c.shape),
                  vec, vec, vec, _const_spec(e.shape), _const_spec(et.shape)],
        out_specs=[row] * 11,
        out_shape=[flat] * 11,
        compiler_params=_params("parallel"),
        name="rw_in",
    )(h, h, h, row1(norm_g), mu, w_r.astype(BF16), w_k.astype(BF16), w_v.astype(BF16), g1p, g2p,
      row1(w0), w1c, w2c, row1(a0), a1c, a2c, row1(k_k), row1(k_a), row1(r_k), e, et)
    r, v, kkn, g, bonus, lwf, kdf, abf, lwb, kdb, abb = outs

    width = pairs * LANES
    n_groups = D_MODEL // width
    span = RW_CHUNKS_PER_STEP * CHUNK
    n_steps = n_rows // span
    fwd_starts = tuple(s // span for s, _, _, _ in seqs)
    bwd_starts = tuple(n_steps - (s + p) // span for s, _, _, p in seqs)
    cf = pl.BlockSpec((span, width), lambda p, c: (c, p))
    cb = pl.BlockSpec((span, width), lambda p, c: (n_steps - 1 - c, p))
    state = pltpu.VMEM((pairs, LANES, LANES), F32)
    y_f, y_b = pl.pallas_call(
        functools.partial(_rw_scan_kernel, fwd_starts=fwd_starts, bwd_starts=bwd_starts, pairs=pairs,
                          chunks=RW_CHUNKS_PER_STEP),
        grid=(n_groups, n_steps),
        in_specs=[cf] * 6 + [cb] * 6,
        out_specs=[cf, cb],
        out_shape=[flat, flat],
        scratch_shapes=[state, state],
        compiler_params=_params("parallel", "arbitrary"),
        name="rw_scan",
    )(r, kdf, v, kkn, abf, lwf, r, kdb, v, kkn, abb, lwb)

    row = _row_spec(tm_out)
    return pl.pallas_call(
        functools.partial(_rw_out_kernel, seqs=seqs, tm=tm_out),
        grid=(n_rows // tm_out,),
        in_specs=[row, row, row, row, row, vec, vec, _const_spec(e.shape), _const_spec(et.shape), sq],
        out_specs=row,
        out_shape=flat,
        compiler_params=_params("parallel"),
        name="rw_out",
    )(y_f, y_b, bonus, g, h, row1(gn_w), row1(gn_b), e, et, w_o.astype(BF16))


def _ffn_kernel(hp_ref, h_ref, hn_ref, g_ref, wu_ref, wv_ref, cw_ref, cb_ref, wo_ref, gf_ref, *out_refs,
                seqs, tm, n_tiles, group_tiles):
    he = _with_halo(hp_ref, h_ref, hn_ref, n_tiles)
    xe = _rms(he, g_ref[...])
    ue = _bdot(xe, wu_ref[...])
    v = _bdot(xe[HALO:HALO + tm], wv_ref[...])
    um1, u0, up1 = _shift_rows(ue, tm)
    cw = cw_ref[...]
    u = um1 * cw[0:1] + u0 * cw[1:2] + up1 * cw[2:3] + cb_ref[...]
    y = he[HALO:HALO + tm] + _bdot(jax.nn.gelu(u) * v, wo_ref[...])
    valid = _valid_rows(pl.program_id(0) * tm, tm, seqs)
    y = jnp.where(valid, y, 0.0)
    if group_tiles is None:
        out_refs[0][...] = y
        return
    y = _rms(y, gf_ref[...])
    i = pl.program_id(0)
    for out_ref, (lo, hi) in zip(out_refs, group_tiles):
        @pl.when((i >= lo) & (i <= hi))
        def _(out_ref=out_ref):
            out_ref[...] = y


def _ffn_layer(h, norm_g, w_in, conv_w, conv_b, w_out, final_g, seqs, tm, groups):
    n_rows = h.shape[0]
    n_tiles = n_rows // tm
    if groups is None:
        group_tiles = None
        out_specs = [_row_spec(tm)]
        out_shape = [jax.ShapeDtypeStruct((n_rows, D_MODEL), F32)]
    else:
        tiling = _group_tiling(seqs, tm, groups)
        group_tiles = [(lo, hi) for lo, hi, _, _, _ in tiling]
        out_specs = [pl.BlockSpec((tm, D_MODEL), block) for _, _, _, block, _ in tiling]
        out_shape = [jax.ShapeDtypeStruct((n_blocks * tm, D_MODEL), F32) for _, _, _, _, n_blocks in tiling]
    prev, cur, nxt = _halo_specs(tm, n_rows)
    vec = _const_spec((1, D_MODEL))
    vecf = _const_spec((1, D_FF))
    return pl.pallas_call(
        functools.partial(_ffn_kernel, seqs=seqs, tm=tm, n_tiles=n_tiles, group_tiles=group_tiles),
        grid=(n_tiles,),
        in_specs=[prev, cur, nxt, vec, _const_spec((D_MODEL, D_FF)), _const_spec((D_MODEL, D_FF)),
                  _const_spec((3, D_FF)), vecf, _const_spec((D_FF, D_MODEL)), vec],
        out_specs=out_specs,
        out_shape=out_shape,
        compiler_params=_params("arbitrary"),
        name="ffn",
    )(h, h, h, norm_g.reshape(1, -1), w_in[:, :D_FF].astype(BF16), w_in[:, D_FF:].astype(BF16),
      conv_w, conv_b.reshape(1, -1), w_out.astype(BF16), final_g.reshape(1, -1))


def _group_tiling(seqs, tm, groups):
    out = []
    for members in groups:
        fronts = tuple(seqs[k][0] // tm for k in members)
        lo = fronts[0]
        hi = (seqs[members[-1]][0] + seqs[members[-1]][3]) // tm - 1
        n_blocks = hi - lo + 1 - len(members)

        def token_block(i, lo=lo, later=fronts[1:], n_blocks=n_blocks):
            skipped = sum((i > f).astype(jnp.int32) for f in later) if later else 0
            return jnp.clip(i - lo - 1 - skipped, 0, n_blocks - 1), 0

        out.append((lo, hi, fronts, token_block, n_blocks))
    return out


def _assemble_kernel(head_ref, *refs, tiling):
    out_ref = refs[-1]
    i = pl.program_id(0)
    for x_ref, (lo, hi, fronts, _, _) in zip(refs[:-1], tiling):
        leading = functools.reduce(jnp.logical_or, [i == f for f in fronts])

        @pl.when((i >= lo) & (i <= hi) & jnp.logical_not(leading))
        def _(x_ref=x_ref):
            out_ref[...] = x_ref[...]

        @pl.when(leading)
        def _():
            out_ref[...] = head_ref[...]


def _assemble(group_tokens, meta, seqs, n_rows, tm, groups):
    tiling = _group_tiling(seqs, tm, groups)
    head = jnp.concatenate([jnp.zeros((tm - N_META, D_MODEL), F32), meta], axis=0)
    return pl.pallas_call(
        functools.partial(_assemble_kernel, tiling=tiling),
        grid=(n_rows // tm,),
        in_specs=[_const_spec((tm, D_MODEL))] + [pl.BlockSpec((tm, D_MODEL), block) for _, _, _, block, _ in tiling],
        out_specs=_row_spec(tm),
        out_shape=jax.ShapeDtypeStruct((n_rows, D_MODEL), F32),
        compiler_params=_params("arbitrary"),
        name="assemble",
    )(head, *group_tokens)


def _layout(token_counts):
    seqs, start = [], 0
    for n in token_counts:
        padded = ROW_ALIGN + -(-n // ROW_ALIGN) * ROW_ALIGN
        seqs.append((start, start + ROW_ALIGN - N_META, N_META + n, padded))
        start += padded
    return tuple(seqs), start


def _trunk(group_tokens, token_counts, p, tm=ROW_ALIGN, tm_rw_in=ROW_ALIGN, rw_pairs=8):
    counts = [t for ts in token_counts for t in ts]
    assert all(t % ROW_ALIGN == 0 for t in counts)
    groups, k = [], 0
    for ts in token_counts:
        groups.append(list(range(k, k + len(ts))))
        k += len(ts)
    seqs, n_rows = _layout(counts)
    h = _assemble(group_tokens, p['meta_tokens'].astype(F32), seqs, n_rows, tm, groups)
    lb_all = jnp.cumsum(jax.nn.softmax(p['hg_lower_bound'].astype(F32), axis=0), axis=0)
    depth = p['norm_mix'].shape[0]
    for layer in range(depth):
        j = layer // N_MIXERS
        if layer % N_MIXERS == 0:
            h = _hgrn2_layer(h, p['norm_mix'][layer], p['hg_w_in'][j], p['hg_w_out'][j], p['hg_onorm'][j],
                             lb_all[j], seqs, tm)
        else:
            h = _rwkv7_layer(h, p['norm_mix'][layer], p['rw_mu'][j], p['rw_w_r'][j], p['rw_w_k'][j],
                             p['rw_w_v'][j], p['rw_w_o'][j], p['rw_w0'][j], p['rw_w1'][j], p['rw_w2'][j],
                             p['rw_a0'][j], p['rw_a1'][j], p['rw_a2'][j], p['rw_g1'][j], p['rw_g2'][j],
                             p['rw_k_k'][j], p['rw_k_a'][j], p['rw_r_k'][j], p['rw_gn_w'][j], p['rw_gn_b'][j],
                             seqs, tm_rw_in, tm, rw_pairs)
        h = _ffn_layer(h, p['norm_ffn'][layer], p['ffn_w_in'][layer], p['ffn_conv_w'][layer],
                       p['ffn_conv_b'][layer], p['ffn_w_out'][layer], p['norm_final'], seqs, tm,
                       groups if layer == depth - 1 else None)
        h = h if layer == depth - 1 else h[0]
    return h


def kernel(x_prompt, x_sample, meta_tokens, norm_mix, norm_ffn, norm_final, hg_w_in, hg_w_out, hg_lower_bound,
           hg_onorm, rw_mu, rw_w_r, rw_w_k, rw_w_v, rw_w_o, rw_w0, rw_w1, rw_w2, rw_a0, rw_a1, rw_a2, rw_g1,
           rw_g2, rw_k_k, rw_k_a, rw_r_k, rw_gn_w, rw_gn_b, ffn_w_in, ffn_conv_w, ffn_conv_b, ffn_w_out):
    p = {
        'meta_tokens': meta_tokens, 'norm_mix': norm_mix, 'norm_ffn': norm_ffn, 'norm_final': norm_final,
        'hg_w_in': hg_w_in, 'hg_w_out': hg_w_out, 'hg_lower_bound': hg_lower_bound, 'hg_onorm': hg_onorm,
        'rw_mu': rw_mu, 'rw_w_r': rw_w_r, 'rw_w_k': rw_w_k, 'rw_w_v': rw_w_v, 'rw_w_o': rw_w_o,
        'rw_w0': rw_w0, 'rw_w1': rw_w1, 'rw_w2': rw_w2, 'rw_a0': rw_a0, 'rw_a1': rw_a1, 'rw_a2': rw_a2,
        'rw_g1': rw_g1, 'rw_g2': rw_g2, 'rw_k_k': rw_k_k, 'rw_k_a': rw_k_a, 'rw_r_k': rw_r_k,
        'rw_gn_w': rw_gn_w, 'rw_gn_b': rw_gn_b,
        'ffn_w_in': ffn_w_in, 'ffn_conv_w': ffn_conv_w, 'ffn_conv_b': ffn_conv_b, 'ffn_w_out': ffn_w_out,
    }
    tokens = [x.reshape(-1, D_MODEL) for x in (x_prompt, x_sample)]
    y_p, y_s = _trunk(tokens, [[x.shape[1]] * x.shape[0] for x in (x_prompt, x_sample)], p)
    return (y_p.reshape(x_prompt.shape), y_s.reshape(x_sample.shape))
```

```python
import functools

import jax
import jax.numpy as jnp
from jax import lax
from jax.experimental import pallas as pl
from jax.experimental.pallas import tpu as pltpu

F32 = jnp.float32
BF16 = jnp.bfloat16
HIGHEST = lax.Precision.HIGHEST

D_MODEL = 1024
D_FF = 2816
N_META = 16
N_MIXERS = 2
HG_HEADS = 8
HG_HEAD_DIM = 128
RW_HEADS = 16
RW_HEAD_DIM = 64
RW_GATE_LORA_PAD = 256
NORM_EPS = 1e-6
RW_GN_EPS = 64e-5
RW_DECAY_SCALE = 0.6065306597126334

LANES = 128
HALO = 8
CHUNK = 64
HG_CHUNKS_PER_STEP = 4
RW_CHUNKS_PER_STEP = 4
ROW_ALIGN = 256
VMEM_LIMIT_BYTES = 56 * 1024 * 1024

NT_DIMS = (((1,), (1,)), ((), ()))


def _const_spec(shape):
    nd = len(shape)
    return pl.BlockSpec(shape, lambda *_: (0,) * nd, pipeline_mode=pl.Buffered(1))


def _params(*sem):
    return pltpu.CompilerParams(dimension_semantics=sem, vmem_limit_bytes=VMEM_LIMIT_BYTES)


def _valid_rows(row0, n, seqs):
    r = row0 + lax.broadcasted_iota(jnp.int32, (n, 1), 0)
    m = None
    for _, first, length, _ in seqs:
        t = (r >= first) & (r < first + length)
        m = t if m is None else (m | t)
    return m


def _rms(x, g):
    return x * lax.rsqrt(jnp.mean(x * x, axis=-1, keepdims=True) + NORM_EPS) * g


def _bdot(a, b):
    return jnp.dot(a.astype(BF16), b.astype(BF16), preferred_element_type=F32)


def _bdot_nt(a, b):
    return lax.dot_general(a.astype(BF16), b.astype(BF16), NT_DIMS, preferred_element_type=F32)


RW_INV_BASE = 4


def _split(x, terms):
    hi = x.astype(BF16)
    if terms == 1:
        return (hi,)
    return hi, (x - hi.astype(F32)).astype(BF16)


def _split3(x):
    x1 = x.astype(BF16)
    r1 = x - x1.astype(F32)
    x2 = r1.astype(BF16)
    x3 = (r1 - x2.astype(F32)).astype(BF16)
    return x1, x2, x3


def _dot_exact_rhs(a_bf16, x):
    x1, x2, x3 = _split3(x)
    d = lambda t: jnp.dot(a_bf16, t, preferred_element_type=F32)
    return d(x1) + d(x2) + d(x3)


def _head_sum(x, e_ref, et_ref):
    def dot2(z, ind):
        z1, z2 = _split(z, 2)
        return (jnp.dot(z1, ind, preferred_element_type=F32) + jnp.dot(z2, ind, preferred_element_type=F32))

    s = dot2(x, e_ref[...])
    return dot2(s, et_ref[...])


def _shift_rows(xe, tm):
    n = xe.shape[0]
    prev = pltpu.roll(xe, 1, axis=0)[HALO:HALO + tm]
    nxt = pltpu.roll(xe, n - 1, axis=0)[HALO:HALO + tm]
    return prev, xe[HALO:HALO + tm], nxt


def _with_halo(hp_ref, h_ref, hn_ref, n_tiles):
    i = pl.program_id(0)
    hp = jnp.where(i > 0, hp_ref[...], 0.0)
    hn = jnp.where(i < n_tiles - 1, hn_ref[...], 0.0)
    return jnp.concatenate([hp, h_ref[...], hn], axis=0)


def _halo_specs(tm, n_rows):
    r = tm // HALO
    last = n_rows // HALO - 1
    prev = pl.BlockSpec((HALO, D_MODEL), lambda i: (jnp.maximum(i * r - 1, 0), 0))
    cur = pl.BlockSpec((tm, D_MODEL), lambda i: (i, 0))
    nxt = pl.BlockSpec((HALO, D_MODEL), lambda i: (jnp.minimum((i + 1) * r, last), 0))
    return prev, cur, nxt


def _row_spec(tm, width=D_MODEL):
    return pl.BlockSpec((tm, width), lambda i: (i, 0))


def _chunk_masks():
    t = lax.broadcasted_iota(jnp.int32, (CHUNK, CHUNK), 0)
    s = lax.broadcasted_iota(jnp.int32, (CHUNK, CHUNK), 1)
    return t, s


def _hg_in_kernel(h_ref, g_ref, w_ref, lb_ref, q_ref, kf_ref, lff_ref, kb_ref, lfb_ref, i_ref, gs_ref,
                  *, seqs, tm):
    hn = _rms(h_ref[...], g_ref[...]).astype(BF16)
    proj = lambda j: jnp.dot(hn, w_ref[:, j * D_MODEL:(j + 1) * D_MODEL], preferred_element_type=F32)
    valid = _valid_rows(pl.program_id(0) * tm, tm, seqs).astype(F32)
    lb = lb_ref[...]
    q = proj(0)
    q_ref[...] = q * jax.nn.sigmoid(q)
    for j, k_ref, lf_ref in ((1, kf_ref, lff_ref), (2, kb_ref, lfb_ref)):
        sg = jax.nn.sigmoid(proj(j))
        lf_ref[...] = jnp.log(lb + (1.0 - lb) * sg) * valid
        k_ref[...] = (1.0 - lb) * (1.0 - sg) * valid
    i_ref[...] = proj(3)
    g = proj(4)
    gs_ref[...] = g * jax.nn.sigmoid(g)


def _hg_chunks(rounds, masks):
    for bodies in rounds:
        for b in bodies:
            m = masks[b['dir']]
            cum = b['cum']
            c_ref = cum[m['mid']:m['mid'] + 1]
            c_last = cum[m['last']:m['last'] + 1]
            q, k = b['q'], b['k']
            q_in, k_in = q * jnp.exp(cum - c_ref), k * jnp.exp(c_ref - cum)
            b.update(q_in=q_in, k_in=k_in, q_st=q_in * jnp.exp(c_ref),
                     k_st=k_in * jnp.exp(c_last - c_ref), g=jnp.exp(c_last))
        for b in bodies:
            b['scores'] = jnp.where(masks[b['dir']]['incl'], _bdot_nt(b['q_in'], b['k_in']), 0.0)
    for bodies in rounds:
        for b in bodies:
            b['st'] = b['st_ref'][b['idx']]
            b['o_ref'][b['rows'], b['sl']] = _bdot(b['scores'], b['iv']) + _bdot_nt(b['q_st'], b['st'])
        for b in bodies:
            b['st_ref'][b['idx']] = b['st'] * b['g'] + _bdot(b['iv'].T, b['k_st'])


def _hg_scan_kernel(q_ref, k_ref, lf_ref, i_ref, *rest, direction, starts, chunks, seqs):
    if direction == 'f':
        ob_ref, gs_ref, h_ref, on_ref, w_ref, out_ref, st_ref, o_ref = rest
    else:
        o_ref, st_ref = rest
    c = pl.program_id(0)

    @pl.when(functools.reduce(jnp.logical_or, [c == s for s in starts]))
    def _():
        st_ref[...] = jnp.zeros(st_ref.shape, F32)

    t, s = _chunk_masks()
    if direction == 'f':
        masks = {'f': dict(incl=s <= t, mid=CHUNK // 2, last=CHUNK - 1)}
        tri = (s <= t).astype(BF16)
    else:
        masks = {'b': dict(incl=s >= t, mid=CHUNK // 2 - 1, last=0)}
        tri = (s >= t).astype(BF16)
    rounds = []
    for j in range(chunks):
        jj = j if direction == 'f' else chunks - 1 - j
        rows = slice(jj * CHUNK, (jj + 1) * CHUNK)
        cum = _dot_exact_rhs(tri, lf_ref[rows, :])
        rounds.append([dict(dir=direction, q=q_ref[rows, sl], k=k_ref[rows, sl], iv=i_ref[rows, sl], cum=cum[:, sl],
                            st_ref=st_ref, idx=hd, o_ref=o_ref, rows=rows, sl=sl)
                       for hd, sl in enumerate(slice(i * HG_HEAD_DIM, (i + 1) * HG_HEAD_DIM)
                                               for i in range(HG_HEADS))])
    _hg_chunks(rounds, masks)
    if direction == 'f':
        span = chunks * CHUNK
        o = o_ref[...] + ob_ref[...]
        parts = []
        for hd in range(HG_HEADS):
            oh = o[:, hd * HG_HEAD_DIM:(hd + 1) * HG_HEAD_DIM]
            parts.append(oh * lax.rsqrt(jnp.mean(oh * oh, axis=-1, keepdims=True) + NORM_EPS))
        on = jnp.concatenate(parts, axis=1) * on_ref[...] * gs_ref[...]
        y = h_ref[...] + _bdot(on, w_ref[...])
        out_ref[...] = jnp.where(_valid_rows(c * span, span, seqs), y, 0.0)


def _hgrn2_layer(h, norm_g, w_in, w_out, onorm_g, lb, seqs, tm):
    n_rows = h.shape[0]
    n_tiles = n_rows // tm
    n_chunks = n_rows // CHUNK
    row = _row_spec(tm)
    vec = _const_spec((1, D_MODEL))
    flat = jax.ShapeDtypeStruct((n_rows, D_MODEL), F32)
    q, kf, lff, kb, lfb, iv, gs = pl.pallas_call(
        functools.partial(_hg_in_kernel, seqs=seqs, tm=tm),
        grid=(n_tiles,),
        in_specs=[row, vec, _const_spec((D_MODEL, 5 * D_MODEL)), vec],
        out_specs=[row] * 7,
        out_shape=[flat] * 7,
        compiler_params=_params("parallel"),
        name="hg_in",
    )(h, norm_g.reshape(1, -1), w_in.astype(BF16), lb.reshape(1, -1))

    span = HG_CHUNKS_PER_STEP * CHUNK
    n_steps = n_rows // span
    fwd_starts = tuple(s // span for s, _, _, _ in seqs)
    bwd_starts = tuple(n_steps - (s + p) // span for s, _, _, p in seqs)
    cf = pl.BlockSpec((span, D_MODEL), lambda c: (c, 0))
    cb = pl.BlockSpec((span, D_MODEL), lambda c: (n_steps - 1 - c, 0))
    state = pltpu.VMEM((HG_HEADS, HG_HEAD_DIM, HG_HEAD_DIM), F32)
    scan = functools.partial(_hg_scan_kernel, chunks=HG_CHUNKS_PER_STEP, seqs=seqs)
    o_b = pl.pallas_call(
        functools.partial(scan, direction='b', starts=bwd_starts),
        grid=(n_steps,),
        in_specs=[cb] * 4,
        out_specs=cb,
        out_shape=flat,
        scratch_shapes=[state],
        compiler_params=_params("arbitrary"),
        name="hg_scan_b",
    )(q, kb, lfb, iv)
    return pl.pallas_call(
        functools.partial(scan, direction='f', starts=fwd_starts),
        grid=(n_steps,),
        in_specs=[cf] * 7 + [vec, _const_spec((D_MODEL, D_MODEL))],
        out_specs=cf,
        out_shape=flat,
        scratch_shapes=[state, pltpu.VMEM((span, D_MODEL), F32)],
        compiler_params=_params("arbitrary"),
        name="hg_scan_f",
    )(q, kf, lff, iv, o_b, gs, h, onorm_g.reshape(1, -1), w_out.astype(BF16))


def _rw_in_kernel(hp_ref, h_ref, hn_ref, g_ref, mu_ref, wr_ref, wk_ref, wv_ref, g1_ref, g2_ref,
                  w0_ref, w1_ref, w2_ref, a0_ref, a1_ref, a2_ref, kk_ref, ka_ref, rk_ref, e_ref, et_ref,
                  r_out, v_out, kkn_out, g_out, bonus_out,
                  lwf_out, kdf_out, abf_out, lwb_out, kdb_out, abb_out, *, seqs, tm, n_tiles):
    xe = _rms(_with_halo(hp_ref, h_ref, hn_ref, n_tiles), g_ref[...])
    xm1, x, xp1 = _shift_rows(xe, tm)
    xx = 0.5 * (xm1 + xp1) - x
    mu = mu_ref[...]
    xr, xw, xk, xv, xa, xg = (x + xx * mu[j:j + 1] for j in range(6))
    r = _bdot(xr, wr_ref[...])
    k = _bdot(xk, wk_ref[...])
    v = _bdot(xv, wv_ref[...])
    g_out[...] = _bdot(jax.nn.sigmoid(_bdot(xg, g1_ref[...])), g2_ref[...])
    z = w0_ref[...] + _bdot(jnp.tanh(_bdot(xw, w1_ref[...])), w2_ref[...])
    lw = -RW_DECAY_SCALE * jax.nn.sigmoid(z)
    a = jax.nn.sigmoid(a0_ref[...] + _bdot(_bdot(xa, a1_ref[...]), a2_ref[...]))
    kkr = k * kk_ref[...]
    ssq = _head_sum(kkr * kkr, e_ref, et_ref)
    kkn = kkr * lax.rsqrt(jnp.maximum(ssq, 1e-24))
    valid = _valid_rows(pl.program_id(0) * tm, tm, seqs).astype(F32)
    ka = ka_ref[...]
    kd_sum = None
    for d, (lw_o, kd_o, ab_o) in enumerate(((lwf_out, kdf_out, abf_out), (lwb_out, kdb_out, abb_out))):
        a_d = a[:, d * D_MODEL:(d + 1) * D_MODEL]
        kd = k * (1.0 + (a_d - 1.0) * ka)
        kd_sum = kd if kd_sum is None else kd_sum + kd
        lw_o[...] = lw[:, d * D_MODEL:(d + 1) * D_MODEL]
        kd_o[...] = kd * valid
        ab_o[...] = kkn * a_d
    r_out[...] = r
    v_out[...] = v
    kkn_out[...] = kkn
    bonus_out[...] = _head_sum(r * kd_sum * rk_ref[...], e_ref, et_ref) * v


class _HalfRows:
    def __init__(self, w, n, lower):
        self.w, self.lower, self.blocks = w, lower, n // (2 * w)
        self.whole = w < HALO

    def pick(self, x):
        if self.whole:
            return x
        off = self.w if self.lower else 0
        return jnp.concatenate([x[2 * j * self.w + off:2 * j * self.w + off + self.w]
                                for j in range(self.blocks)], axis=0)

    def spread(self, y):
        if self.whole:
            return y
        zero = jnp.zeros((self.w, y.shape[1]), y.dtype)
        parts = []
        for j in range(self.blocks):
            piece = y[j * self.w:(j + 1) * self.w]
            parts += [zero, piece] if self.lower else [piece, zero]
        return jnp.concatenate(parts, axis=0)


def _rw_prepare(bodies, masks, eye, lane_a):
    stack = lambda z_: jnp.concatenate([jnp.where(lane_a, z_, 0.0), jnp.where(lane_a, 0.0, z_)], axis=0)
    cast = lambda z_: z_.astype(BF16)
    dot = lambda x_, y_: jnp.dot(x_, y_, preferred_element_type=F32)
    n = 2 * CHUNK
    for b in bodies:
        m = masks[b['dir']]
        c, lw = b['c'], b['lw']
        e = c - lw
        c_mid = c[m['mid']:m['mid'] + 1]
        c_last = c[m['last']:m['last'] + 1]
        beta = -b['ab']
        dec_out = jnp.exp(c_mid - c)
        dec_end = jnp.exp(c_last - c)
        vs = stack(b['v'])
        cols = jnp.concatenate([beta * dec_out] * 2 + [b['k'] * dec_out] * 2, axis=0)
        b.update(
            lhs=cast(jnp.concatenate([stack(b['al'] * jnp.exp(e - c_mid)), stack(b['r'] * jnp.exp(c - c_mid))],
                                     axis=0)),
            rhs=cast(cols.T),
            vs=cast(vs), vs_t=cast(vs.T), al_h=cast(stack(b['al'] * jnp.exp(e))),
            r_h=cast(stack(b['r'] * jnp.exp(c))), b_h=cast(stack(beta * dec_end)),
            k_h=cast(stack(b['k'] * dec_end)), g=jnp.exp(c_last))
    for b in bodies:
        b['sc'] = dot(b['lhs'], b['rhs'])
    for b in bodies:
        m = masks[b['dir']]
        sc_a, sc_r = b['sc'][:n], b['sc'][n:]
        b.update(
            a_d=jnp.where(m['diag'], sc_a[:, :n], 0.0),
            a_off=[jnp.where(mo, sc_a[:, :n], 0.0) for mo in m['off']],
            a_ak=cast(jnp.where(m['strict'], sc_a[:, n:], 0.0)),
            a_rb=cast(jnp.where(m['incl'], sc_r[:, :n], 0.0)), a_rk=cast(jnp.where(m['incl'], sc_r[:, n:], 0.0)))
    for b in bodies:
        b['pw'] = cast(b['a_d'])
        b['inv'] = eye + b['a_d']
    for _ in range(RW_INV_BASE.bit_length() - 2):
        for b in bodies:
            b['pw'] = cast(dot(b['pw'], b['pw']))
        for b in bodies:
            b['inv'] = b['inv'] + dot(cast(b['inv']), b['pw'])
    for level, w in enumerate(masks['widths']):
        for b in bodies:
            b['half'] = _HalfRows(w, n, lower=b['dir'] == 'f')
            b['t'] = dot(cast(b['half'].pick(b['a_off'][level])), cast(b['inv']))
        for b in bodies:
            upd = dot(cast(b['half'].pick(b['inv'])), cast(b['half'].spread(b['t'])))
            b['inv'] = b['inv'] + b['half'].spread(upd)
    for b in bodies:
        b['inv16'] = cast(b['inv'])


def _rw_advance(bodies):
    for b in bodies:
        b['ht'] = b['ht_ref'][b['idx']]
        b['h16'] = b['ht'].T.astype(BF16)
        b['x'] = jnp.dot(jnp.concatenate([b['al_h'], b['a_ak']], axis=1),
                         jnp.concatenate([b['h16'], b['vs']], axis=0), preferred_element_type=F32)
    for b in bodies:
        b['u'] = jnp.dot(b['inv16'], b['x'].astype(BF16), preferred_element_type=F32)
    for b in bodies:
        y = jnp.dot(jnp.concatenate([b['r_h'], b['a_rb'], b['a_rk']], axis=1),
                    jnp.concatenate([b['h16'], b['u'].astype(BF16), b['vs']], axis=0),
                    preferred_element_type=F32)
        b['y_ref'][b['rows'], b['sl']] = y[:CHUNK] + y[CHUNK:]
    for b in bodies:
        b['ht_ref'][b['idx']] = b['ht'] * b['g'] + jnp.dot(
            jnp.concatenate([b['u'].T.astype(BF16), b['vs_t']], axis=1),
            jnp.concatenate([b['b_h'], b['k_h']], axis=0), preferred_element_type=F32)


def _rw_scan_kernel(rf_ref, kf_ref, vf_ref, alf_ref, abf_ref, lwf_ref,
                    rb_ref, kb_ref, vb_ref, alb_ref, abb_ref, lwb_ref,
                    yf_ref, yb_ref, hf_ref, hb_ref, *, fwd_starts, bwd_starts, pairs, chunks):
    c = pl.program_id(1)

    def reset_at(ref, steps):
        hit = functools.reduce(jnp.logical_or, [c == s for s in steps])

        @pl.when(hit)
        def _():
            ref[...] = jnp.zeros(ref.shape, F32)

    reset_at(hf_ref, fwd_starts)
    reset_at(hb_ref, bwd_starts)
    t, s = _chunk_masks()
    n = 2 * CHUNK
    ti = lax.broadcasted_iota(jnp.int32, (n, n), 0)
    si = lax.broadcasted_iota(jnp.int32, (n, n), 1)
    same = lambda width: (ti // width) == (si // width)
    eye = (ti == si).astype(F32)
    lane_a = lax.broadcasted_iota(jnp.int32, (CHUNK, LANES), 1) < RW_HEAD_DIM

    widths = [RW_INV_BASE << j for j in range((CHUNK // RW_INV_BASE).bit_length() - 1)]

    def dir_masks(before, before_incl, mid, last):
        return dict(mid=mid, last=last,
                    strict=same(CHUNK) & before, incl=same(CHUNK) & before_incl,
                    diag=same(RW_INV_BASE) & before,
                    off=[same(2 * w) & ~same(w) & before for w in widths])

    masks = {'f': dir_masks(si < ti, si <= ti, CHUNK // 2, CHUNK - 1),
             'b': dir_masks(si > ti, si >= ti, CHUNK // 2 - 1, 0), 'widths': widths}
    tri_f, tri_b = (s <= t).astype(BF16), (s >= t).astype(BF16)
    rounds = []
    for j in range(chunks):
        rf = slice(j * CHUNK, (j + 1) * CHUNK)
        rb = slice((chunks - 1 - j) * CHUNK, (chunks - j) * CHUNK)
        cum_f = _dot_exact_rhs(tri_f, lwf_ref[rf, :])
        cum_b = _dot_exact_rhs(tri_b, lwb_ref[rb, :])
        bodies = []
        for p in range(pairs):
            sl = slice(p * LANES, (p + 1) * LANES)
            bodies.append(dict(dir='f', r=rf_ref[rf, sl], k=kf_ref[rf, sl], v=vf_ref[rf, sl], al=alf_ref[rf, sl],
                               ab=abf_ref[rf, sl], lw=lwf_ref[rf, sl], c=cum_f[:, sl], ht_ref=hf_ref, idx=p,
                               y_ref=yf_ref, rows=rf, sl=sl))
            bodies.append(dict(dir='b', r=rb_ref[rb, sl], k=kb_ref[rb, sl], v=vb_ref[rb, sl], al=alb_ref[rb, sl],
                               ab=abb_ref[rb, sl], lw=lwb_ref[rb, sl], c=cum_b[:, sl], ht_ref=hb_ref, idx=p,
                               y_ref=yb_ref, rows=rb, sl=sl))
        _rw_prepare(bodies, masks, eye, lane_a)
        rounds.append(bodies)
    for bodies in rounds:
        _rw_advance(bodies)


def _rw_out_kernel(yf_ref, yb_ref, bonus_ref, g_ref, h_ref, gnw_ref, gnb_ref, e_ref, et_ref, w_ref, out_ref,
                   *, seqs, tm):
    y = yf_ref[...] + yb_ref[...]
    inv_n = 1.0 / RW_HEAD_DIM
    mean = _head_sum(y, e_ref, et_ref) * inv_n
    dlt = y - mean
    var = _head_sum(dlt * dlt, e_ref, et_ref) * inv_n
    yn = dlt * lax.rsqrt(var + RW_GN_EPS) * gnw_ref[...] + gnb_ref[...]
    out = (yn + bonus_ref[...]) * g_ref[...]
    res = h_ref[...] + _bdot(out, w_ref[...])
    valid = _valid_rows(pl.program_id(0) * tm, tm, seqs)
    out_ref[...] = jnp.where(valid, res, 0.0)


def _rwkv7_layer(h, norm_g, mu, w_r, w_k, w_v, w_o, w0, w1, w2, a0, a1, a2, g1, g2, k_k, k_a, r_k, gn_w, gn_b,
                 seqs, tm_in, tm_out, pairs):
    n_rows = h.shape[0]
    n_chunks = n_rows // CHUNK
    flat = jax.ShapeDtypeStruct((n_rows, D_MODEL), F32)
    vec = _const_spec((1, D_MODEL))
    vec2 = _const_spec((1, 2 * D_MODEL))
    row1 = lambda t: t.reshape(1, -1)

    def cat_dirs(w):
        return jnp.concatenate([w[0], w[1]], axis=1).astype(BF16)

    def blockdiag_dirs(w):
        z = jnp.zeros_like(w[0])
        return jnp.concatenate([jnp.concatenate([w[0], z], axis=1),
                                jnp.concatenate([z, w[1]], axis=1)], axis=0).astype(BF16)

    lora = g1.shape[1]
    g1p = jnp.pad(g1, ((0, 0), (0, RW_GATE_LORA_PAD - lora))).astype(BF16)
    g2p = jnp.pad(g2, ((0, RW_GATE_LORA_PAD - lora), (0, 0))).astype(BF16)
    w1c, w2c, a1c, a2c = cat_dirs(w1), blockdiag_dirs(w2), cat_dirs(a1), blockdiag_dirs(a2)
    e = (lax.broadcasted_iota(jnp.int32, (D_MODEL, LANES), 0) // RW_HEAD_DIM
         == lax.broadcasted_iota(jnp.int32, (D_MODEL, LANES), 1)).astype(BF16)
    et = e.T

    n_tiles = n_rows // tm_in
    prev, cur, nxt = _halo_specs(tm_in, n_rows)
    row = _row_spec(tm_in)
    sq = _const_spec((D_MODEL, D_MODEL))
    outs = pl.pallas_call(
        functools.partial(_rw_in_kernel, seqs=seqs, tm=tm_in, n_tiles=n_tiles),
        grid=(n_tiles,),
        in_specs=[prev, cur, nxt, vec, _const_spec((6, D_MODEL)), sq, sq, sq,
                  _const_spec(g1p.shape), _const_spec(g2p.shape),
                  vec2, _const_spec(w1c.shape), _const_spec(w2c.shape),
                  vec2, _const_spec(a1c.shape), _const_spec(a2c.shape),
                  vec, vec, vec, _const_spec(e.shape), _const_spec(et.shape)],
        out_specs=[row] * 11,
        out_shape=[flat] * 11,
        compiler_params=_params("parallel"),
        name="rw_in",
    )(h, h, h, row1(norm_g), mu, w_r.astype(BF16), w_k.astype(BF16), w_v.astype(BF16), g1p, g2p,
      row1(w0), w1c, w2c, row1(a0), a1c, a2c, row1(k_k), row1(k_a), row1(r_k), e, et)
    r, v, kkn, g, bonus, lwf, kdf, abf, lwb, kdb, abb = outs

    width = pairs * LANES
    n_groups = D_MODEL // width
    span = RW_CHUNKS_PER_STEP * CHUNK
    n_steps = n_rows // span
    fwd_starts = tuple(s // span for s, _, _, _ in seqs)
    bwd_starts = tuple(n_steps - (s + p) // span for s, _, _, p in seqs)
    cf = pl.BlockSpec((span, width), lambda p, c: (c, p))
    cb = pl.BlockSpec((span, width), lambda p, c: (n_steps - 1 - c, p))
    state = pltpu.VMEM((pairs, LANES, LANES), F32)
    y_f, y_b = pl.pallas_call(
        functools.partial(_rw_scan_kernel, fwd_starts=fwd_starts, bwd_starts=bwd_starts, pairs=pairs,
                          chunks=RW_CHUNKS_PER_STEP),
        grid=(n_groups, n_steps),
        in_specs=[cf] * 6 + [cb] * 6,
        out_specs=[cf, cb],
        out_shape=[flat, flat],
        scratch_shapes=[state, state],
        compiler_params=_params("parallel", "arbitrary"),
        name="rw_scan",
    )(r, kdf, v, kkn, abf, lwf, r, kdb, v, kkn, abb, lwb)

    row = _row_spec(tm_out)
    return pl.pallas_call(
        functools.partial(_rw_out_kernel, seqs=seqs, tm=tm_out),
        grid=(n_rows // tm_out,),
        in_specs=[row, row, row, row, row, vec, vec, _const_spec(e.shape), _const_spec(et.shape), sq],
        out_specs=row,
        out_shape=flat,
        compiler_params=_params("parallel"),
        name="rw_out",
    )(y_f, y_b, bonus, g, h, row1(gn_w), row1(gn_b), e, et, w_o.astype(BF16))


def _ffn_kernel(hp_ref, h_ref, hn_ref, g_ref, wu_ref, wv_ref, cw_ref, cb_ref, wo_ref, gf_ref, *out_refs,
                seqs, tm, n_tiles, group_tiles):
    he = _with_halo(hp_ref, h_ref, hn_ref, n_tiles)
    xe = _rms(he, g_ref[...])
    ue = _bdot(xe, wu_ref[...])
    v = _bdot(xe[HALO:HALO + tm], wv_ref[...])
    um1, u0, up1 = _shift_rows(ue, tm)
    cw = cw_ref[...]
    u = um1 * cw[0:1] + u0 * cw[1:2] + up1 * cw[2:3] + cb_ref[...]
    y = he[HALO:HALO + tm] + _bdot(jax.nn.gelu(u) * v, wo_ref[...])
    valid = _valid_rows(pl.program_id(0) * tm, tm, seqs)
    y = jnp.where(valid, y, 0.0)
    if group_tiles is None:
        out_refs[0][...] = y
        return
    y = _rms(y, gf_ref[...])
    i = pl.program_id(0)
    for out_ref, (lo, hi) in zip(out_refs, group_tiles):
        @pl.when((i >= lo) & (i <= hi))
        def _(out_ref=out_ref):
            out_ref[...] = y


def _ffn_layer(h, norm_g, w_in, conv_w, conv_b, w_out, final_g, seqs, tm, groups):
    n_rows = h.shape[0]
    n_tiles = n_rows // tm
    if groups is None:
        group_tiles = None
        out_specs = [_row_spec(tm)]
        out_shape = [jax.ShapeDtypeStruct((n_rows, D_MODEL), F32)]
    else:
        group_tiles, out_specs, out_shape = [], [], []
        for members in groups:
            fronts = [seqs[k][0] // tm for k in members]
            lo = fronts[0]
            hi = (seqs[members[-1]][0] + seqs[members[-1]][3]) // tm - 1
            n_blocks = hi - lo + 1 - len(members)

            def token_block(i, lo=lo, later=tuple(fronts[1:]), n_blocks=n_blocks):
                skipped = sum((i > f).astype(jnp.int32) for f in later) if later else 0
                return jnp.clip(i - lo - 1 - skipped, 0, n_blocks - 1), 0

            group_tiles.append((lo, hi))
            out_specs.append(pl.BlockSpec((tm, D_MODEL), token_block))
            out_shape.append(jax.ShapeDtypeStruct((n_blocks * tm, D_MODEL), F32))
    prev, cur, nxt = _halo_specs(tm, n_rows)
    vec = _const_spec((1, D_MODEL))
    vecf = _const_spec((1, D_FF))
    return pl.pallas_call(
        functools.partial(_ffn_kernel, seqs=seqs, tm=tm, n_tiles=n_tiles, group_tiles=group_tiles),
        grid=(n_tiles,),
        in_specs=[prev, cur, nxt, vec, _const_spec((D_MODEL, D_FF)), _const_spec((D_MODEL, D_FF)),
                  _const_spec((3, D_FF)), vecf, _const_spec((D_FF, D_MODEL)), vec],
        out_specs=out_specs,
        out_shape=out_shape,
        compiler_params=_params("arbitrary"),
        name="ffn",
    )(h, h, h, norm_g.reshape(1, -1), w_in[:, :D_FF].astype(BF16), w_in[:, D_FF:].astype(BF16),
      conv_w, conv_b.reshape(1, -1), w_out.astype(BF16), final_g.reshape(1, -1))


def _layout(token_counts):
    seqs, start = [], 0
    for n in token_counts:
        padded = ROW_ALIGN + -(-n // ROW_ALIGN) * ROW_ALIGN
        seqs.append((start, start + ROW_ALIGN - N_META, N_META + n, padded))
        start += padded
    return tuple(seqs), start


def _trunk(xs, p, groups=None, tm=ROW_ALIGN, tm_rw_in=ROW_ALIGN, rw_pairs=8):
    assert all(x.shape[0] % ROW_ALIGN == 0 for x in xs)
    groups = [[k] for k in range(len(xs))] if groups is None else groups
    seqs, n_rows = _layout([x.shape[0] for x in xs])
    meta = p['meta_tokens'].astype(F32)
    pieces = []
    for x in xs:
        pieces += [jnp.zeros((ROW_ALIGN - N_META, D_MODEL), F32), meta, x]
    h = jnp.concatenate(pieces, axis=0)
    lb_all = jnp.cumsum(jax.nn.softmax(p['hg_lower_bound'].astype(F32), axis=0), axis=0)
    depth = p['norm_mix'].shape[0]
    for layer in range(depth):
        j = layer // N_MIXERS
        if layer % N_MIXERS == 0:
            h = _hgrn2_layer(h, p['norm_mix'][layer], p['hg_w_in'][j], p['hg_w_out'][j], p['hg_onorm'][j],
                             lb_all[j], seqs, tm)
        else:
            h = _rwkv7_layer(h, p['norm_mix'][layer], p['rw_mu'][j], p['rw_w_r'][j], p['rw_w_k'][j],
                             p['rw_w_v'][j], p['rw_w_o'][j], p['rw_w0'][j], p['rw_w1'][j], p['rw_w2'][j],
                             p['rw_a0'][j], p['rw_a1'][j], p['rw_a2'][j], p['rw_g1'][j], p['rw_g2'][j],
                             p['rw_k_k'][j], p['rw_k_a'][j], p['rw_r_k'][j], p['rw_gn_w'][j], p['rw_gn_b'][j],
                             seqs, tm_rw_in, tm, rw_pairs)
        h = _ffn_layer(h, p['norm_ffn'][layer], p['ffn_w_in'][layer], p['ffn_conv_w'][layer],
                       p['ffn_conv_b'][layer], p['ffn_w_out'][layer], p['norm_final'], seqs, tm,
                       groups if layer == depth - 1 else None)
        h = h if layer == depth - 1 else h[0]
    return h


def kernel(x_prompt, x_sample, meta_tokens, norm_mix, norm_ffn, norm_final, hg_w_in, hg_w_out, hg_lower_bound,
           hg_onorm, rw_mu, rw_w_r, rw_w_k, rw_w_v, rw_w_o, rw_w0, rw_w1, rw_w2, rw_a0, rw_a1, rw_a2, rw_g1,
           rw_g2, rw_k_k, rw_k_a, rw_r_k, rw_gn_w, rw_gn_b, ffn_w_in, ffn_conv_w, ffn_conv_b, ffn_w_out):
    p = {
        'meta_tokens': meta_tokens, 'norm_mix': norm_mix, 'norm_ffn': norm_ffn, 'norm_final': norm_final,
        'hg_w_in': hg_w_in, 'hg_w_out': hg_w_out, 'hg_lower_bound': hg_lower_bound, 'hg_onorm': hg_onorm,
        'rw_mu': rw_mu, 'rw_w_r': rw_w_r, 'rw_w_k': rw_w_k, 'rw_w_v': rw_w_v, 'rw_w_o': rw_w_o,
        'rw_w0': rw_w0, 'rw_w1': rw_w1, 'rw_w2': rw_w2, 'rw_a0': rw_a0, 'rw_a1': rw_a1, 'rw_a2': rw_a2,
        'rw_g1': rw_g1, 'rw_g2': rw_g2, 'rw_k_k': rw_k_k, 'rw_k_a': rw_k_a, 'rw_r_k': rw_r_k,
        'rw_gn_w': rw_gn_w, 'rw_gn_b': rw_gn_b,
        'ffn_w_in': ffn_w_in, 'ffn_conv_w': ffn_conv_w, 'ffn_conv_b': ffn_conv_b, 'ffn_w_out': ffn_w_out,
    }
    xs = [x_prompt[b] for b in range(x_prompt.shape[0])] + [x_sample[b] for b in range(x_sample.shape[0])]
    n_p, n_s = x_prompt.shape[0], x_sample.shape[0]
    y_p, y_s = _trunk(xs, p, groups=[list(range(n_p)), list(range(n_p, n_p + n_s))])
    return (y_p.reshape(x_prompt.shape), y_s.reshape(x_sample.shape))
```
